```python
import math
import jax, jax.numpy as jnp
from jax import lax
import numpy as np

D_MODEL = 1024
BATCH = 4
SEQ = 4096
DEPTH = 1

CHUNK = 64
MIX_WIDTH = D_MODEL
SB_WIDTH = D_MODEL // 2
SB_HEADS = 8
SB_HEAD_DIM = SB_WIDTH // SB_HEADS
SB_Q_BLOCK = 128
GLA_WIDTH = MIX_WIDTH - SB_WIDTH
GLA_HEADS = 4
GLA_KEY_DIM = GLA_WIDTH // 2 // GLA_HEADS
GLA_VAL_DIM = GLA_WIDTH // GLA_HEADS
GLA_GATE_RANK = 16
GLA_TAU = 16.0
D_FF = 2816
CONV_WIDTH = 3
LN_EPS = 1e-5
RMS_EPS = 1e-6
DN_ALPHA = (2.0 * DEPTH) ** 0.25
DN_BETA = (8.0 * DEPTH) ** -0.25

IN_SIZES = (SB_WIDTH, SB_WIDTH, SB_WIDTH,
            GLA_HEADS * GLA_KEY_DIM, GLA_HEADS * GLA_KEY_DIM, GLA_WIDTH, GLA_WIDTH,
            GLA_GATE_RANK)
IN_SPLITS = tuple(int(v) for v in np.cumsum(IN_SIZES)[:-1])
IN_WIDTH = int(sum(IN_SIZES))

kernel_name = "hybrid_stickbreak_gla_convffn_deepnorm"


def layer_norm(x, g, b):
    xf = x.astype(jnp.float32)
    mu = jnp.mean(xf, axis=-1, keepdims=True)
    var = jnp.mean(jnp.square(xf - mu), axis=-1, keepdims=True)
    y = (xf - mu) * lax.rsqrt(var + LN_EPS) * g.astype(jnp.float32) + b.astype(jnp.float32)
    return y.astype(x.dtype)


def stick_breaking_attention(q, k, v):
    S = q.shape[2]
    scale = SB_HEAD_DIM ** -0.5
    outs = []
    for q0 in range(0, S, SB_Q_BLOCK):
        L = q0 + SB_Q_BLOCK
        qb = q[:, :, q0:L].astype(jnp.float32)
        kb = k[:, :, :L].astype(jnp.float32)
        vb = v[:, :, :L].astype(jnp.float32)
        z = jnp.einsum("bhqd,bhkd->bhqk", qb, kb) * scale
        qpos = q0 + jnp.arange(SB_Q_BLOCK)[:, None]
        kpos = jnp.arange(L)[None, :]
        strict = kpos < qpos
        log_1m = jnp.where(strict, jax.nn.log_sigmoid(-z), 0.0)
        suffix = lax.cumsum(log_1m, axis=3, reverse=True) - log_1m
        log_w = jax.nn.log_sigmoid(z) + suffix
        w = jnp.where(strict, jnp.exp(log_w), 0.0)
        outs.append(jnp.einsum("bhqk,bhkd->bhqd", w, vb))
    return jnp.concatenate(outs, axis=2).astype(v.dtype)


def gla_chunked(q, k, v, log_a):
    B, S, H, Dk = q.shape
    Dv = v.shape[-1]
    N = S // CHUNK
    f32 = jnp.float32
    q = (q.astype(f32) * Dk ** -0.5).reshape(B, N, CHUNK, H, Dk)
    k = k.astype(f32).reshape(B, N, CHUNK, H, Dk)
    v = v.astype(f32).reshape(B, N, CHUNK, H, Dv)
    g = log_a.astype(f32).reshape(B, N, CHUNK, H, Dk)
    b = jnp.cumsum(g, axis=2)
    b_ref = b[:, :, CHUNK // 2 - 1:CHUNK // 2]
    q_in = q * jnp.exp(b - b_ref)
    k_in = k * jnp.exp(b_ref - b)
    scores = jnp.einsum("bnthd,bnshd->bnhts", q_in, k_in)
    causal = jnp.tril(jnp.ones((CHUNK, CHUNK), dtype=bool))
    scores = jnp.where(causal, scores, 0.0)
    o_intra = jnp.einsum("bnhts,bnshv->bnthv", scores, v)
    b_last = b[:, :, -1]
    k_dec = k * jnp.exp(b_last[:, :, None] - b)
    chunk_upd = jnp.einsum("bnshk,bnshv->bnhkv", k_dec, v)
    decay = jnp.exp(b_last)

    def step(state, inp):
        dec, upd = inp
        return dec[..., None] * state + upd, state

    init = jnp.zeros((B, H, Dk, Dv), f32)
    _, prev = lax.scan(step, init, (jnp.moveaxis(decay, 1, 0), jnp.moveaxis(chunk_upd, 1, 0)))
    prev = jnp.moveaxis(prev, 0, 1)
    o_inter = jnp.einsum("bnthk,bnhkv->bnthv", q * jnp.exp(b), prev)
    return (o_intra + o_inter).reshape(B, S, H, Dv)


def causal_depthwise_conv(u, w, bias):
    C = u.shape[-1]
    y = lax.conv_general_dilated(
        u, w[:, None, :].astype(u.dtype), window_strides=(1,),
        padding=[(CONV_WIDTH - 1, 0)], dimension_numbers=("NWC", "WIO", "NWC"),
        feature_group_count=C)
    return y + bias


def hybrid_layer(x, w_in, gate_up, gate_bias, gla_norm_g, w_out, ln1_g, ln1_b,
                 w_up, conv_w, conv_b, w_down, ln2_g, ln2_b):
    B, S, _ = x.shape
    proj = x @ w_in
    sb_q, sb_k, sb_v, gq, gk, gv, gg, ga = jnp.split(proj, IN_SPLITS, axis=-1)

    def to_heads(t):
        return t.reshape(B, S, SB_HEADS, SB_HEAD_DIM).transpose(0, 2, 1, 3)

    sb_o = stick_breaking_attention(to_heads(sb_q), to_heads(sb_k), to_heads(sb_v))
    sb_o = sb_o.transpose(0, 2, 1, 3).reshape(B, S, SB_WIDTH)

    log_a = jax.nn.log_sigmoid((ga @ gate_up + gate_bias).astype(jnp.float32)) / GLA_TAU
    o = gla_chunked(gq.reshape(B, S, GLA_HEADS, GLA_KEY_DIM),
                    gk.reshape(B, S, GLA_HEADS, GLA_KEY_DIM),
                    gv.reshape(B, S, GLA_HEADS, GLA_VAL_DIM),
                    log_a.reshape(B, S, GLA_HEADS, GLA_KEY_DIM))
    o = o * lax.rsqrt(jnp.mean(jnp.square(o), axis=-1, keepdims=True) + RMS_EPS)
    o = o * gla_norm_g.astype(jnp.float32)
    gla_o = (o.reshape(B, S, GLA_WIDTH) * jax.nn.silu(gg.astype(jnp.float32))).astype(x.dtype)

    mix = jnp.concatenate([sb_o, gla_o], axis=-1) @ w_out
    h = layer_norm(DN_ALPHA * x + mix, ln1_g, ln1_b)

    u = causal_depthwise_conv(h @ w_up, conv_w, conv_b)
    a, c = jnp.split(u, 2, axis=-1)
    f = (jax.nn.gelu(a, approximate=False) * c) @ w_down
    return layer_norm(DN_ALPHA * h + f, ln2_g, ln2_b)


def setup_inputs(seed: int = 0) -> dict:
    key = jax.random.key(seed)
    ks = jax.random.split(key, 24)
    d = D_MODEL
    nrm = lambda k, shape, s: jax.random.normal(k, shape, jnp.float32) * s
    in_scale = d ** -0.5
    pieces = [
        nrm(ks[1], (DEPTH, d, SB_WIDTH), in_scale),
        nrm(ks[2], (DEPTH, d, SB_WIDTH), in_scale),
        nrm(ks[3], (DEPTH, d, SB_WIDTH), in_scale * DN_BETA),
        nrm(ks[4], (DEPTH, d, GLA_HEADS * GLA_KEY_DIM), in_scale),
        nrm(ks[5], (DEPTH, d, GLA_HEADS * GLA_KEY_DIM), in_scale),
        nrm(ks[6], (DEPTH, d, GLA_WIDTH), in_scale * DN_BETA),
        nrm(ks[7], (DEPTH, d, GLA_WIDTH), in_scale),
        nrm(ks[8], (DEPTH, d, GLA_GATE_RANK), in_scale),
    ]
    return {
        "x": nrm(ks[0], (BATCH, SEQ, d), 1.0),
        "w_in": jnp.concatenate(pieces, axis=-1),
        "gate_up": nrm(ks[9], (DEPTH, GLA_GATE_RANK, GLA_HEADS * GLA_KEY_DIM), GLA_GATE_RANK ** -0.5),
        "gate_bias": nrm(ks[10], (DEPTH, GLA_HEADS * GLA_KEY_DIM), 0.1),
        "gla_norm_g": 1.0 + nrm(ks[11], (DEPTH, GLA_VAL_DIM), 0.02),
        "w_out": nrm(ks[12], (DEPTH, MIX_WIDTH, d), MIX_WIDTH ** -0.5 * DN_BETA),
        "ln1_g": 1.0 + nrm(ks[13], (DEPTH, d), 0.02),
        "ln1_b": nrm(ks[14], (DEPTH, d), 0.02),
        "w_up": nrm(ks[15], (DEPTH, d, 2 * D_FF), d ** -0.5 * DN_BETA),
        "conv_w": nrm(ks[16], (DEPTH, CONV_WIDTH, 2 * D_FF), CONV_WIDTH ** -0.5),
        "conv_b": nrm(ks[17], (DEPTH, 2 * D_FF), 0.02),
        "w_down": nrm(ks[18], (DEPTH, D_FF, d), D_FF ** -0.5 * DN_BETA),
        "ln2_g": 1.0 + nrm(ks[19], (DEPTH, d), 0.02),
        "ln2_b": nrm(ks[20], (DEPTH, d), 0.02),
    }


def reference(x, w_in, gate_up, gate_bias, gla_norm_g, w_out, ln1_g, ln1_b,
              w_up, conv_w, conv_b, w_down, ln2_g, ln2_b):
    for l in range(DEPTH):
        x = hybrid_layer(x, w_in[l], gate_up[l], gate_bias[l], gla_norm_g[l], w_out[l],
                         ln1_g[l], ln1_b[l], w_up[l], conv_w[l], conv_b[l], w_down[l],
                         ln2_g[l], ln2_b[l])
    return x
```

```python
import functools

import jax
import jax.numpy as jnp
from jax import lax
from jax.experimental import pallas as pl
from jax.experimental.pallas import tpu as pltpu

F32 = jnp.float32
BF16 = jnp.bfloat16

D_MODEL = 1024
SB_WIDTH = 512
SB_HEADS = 8
SB_HEAD_DIM = 64
GLA_WIDTH = 512
GLA_HEADS = 4
GLA_KEY_DIM = 64
GLA_VAL_DIM = 128
GLA_KEYS = GLA_HEADS * GLA_KEY_DIM
GLA_GATE_RANK = 16
GLA_TAU = 16.0
GLA_CHUNK = 64
D_FF = 2816
CONV_WIDTH = 3
LN_EPS = 1e-5
RMS_EPS = 1e-6
MAIN_WIDTH = 3 * SB_WIDTH + 2 * GLA_KEYS + 2 * GLA_WIDTH

LANES = 128
BF16_SUBLANES = 16
VMEM_LIMIT_BYTES = 56 * 1024 * 1024

PROJ_ROWS = 512
SB_BLOCK = 256
GLA_ROWS = 512
MIX_ROWS = 512
FFN_ROWS = 512
FFN_COLS = 1408
FFN_CHUNKS = ((0, 256), (256, 256), (512, 256), (768, 256), (1024, 256), (1280, 128))
CONV_HALO = BF16_SUBLANES

SB_DEAD_LOG_WEIGHT = -90.0

NT_DIMS = (((1,), (1,)), ((), ()))
TN_DIMS = (((0,), (0,)), ((), ()))


def _split_bf16(x):
    hi = x.astype(BF16)
    lo = (x - hi.astype(F32)).astype(BF16)
    return hi, lo


def _dot(a, b):
    return jnp.dot(a, b, preferred_element_type=F32)


def _dot3(a, b):
    a_hi, a_lo = _split_bf16(a)
    b_hi, b_lo = _split_bf16(b)
    return _dot(a_hi, b_hi) + _dot(a_lo, b_hi) + _dot(a_hi, b_lo)


def _softplus(z):
    return jnp.maximum(z, 0.0) + jnp.log(1.0 + jnp.exp(-jnp.abs(z)))


def _layer_norm(y, g, b):
    mu = jnp.mean(y, axis=-1, keepdims=True)
    d = y - mu
    var = jnp.mean(d * d, axis=-1, keepdims=True)
    return d * lax.rsqrt(var + LN_EPS) * g + b


def _proj_kernel(x_ref, w_ref, wga_ref, gup_ref, gbias_ref,
                 q_ref, k_ref, v_ref, gq_ref, gk_ref, gv_ref, gg_ref, la_ref):
    x = x_ref[...]
    xb = x.astype(BF16)
    col = 0
    for out_ref, scale in ((q_ref, SB_HEAD_DIM ** -0.5), (k_ref, None), (v_ref, None),
                           (gq_ref, GLA_KEY_DIM ** -0.5), (gk_ref, None), (gv_ref, None),
                           (gg_ref, None)):
        width = out_ref.shape[1]
        y = _dot(xb, w_ref[:, col:col + width])
        if scale is not None:
            y = y * scale
        out_ref[...] = y.astype(out_ref.dtype)
        col += width
    ga = _dot3(x, wga_ref[...])
    logits = _dot3(ga, gup_ref[...]) + gbias_ref[...]
    la_ref[...] = -_softplus(-logits) * (1.0 / GLA_TAU)


def _input_projection(x, w_main, w_ga, gate_up, gate_bias):
    tokens = x.shape[0]
    rows = PROJ_ROWS
    widths = (SB_WIDTH, SB_WIDTH, SB_WIDTH, GLA_KEYS, GLA_KEYS, GLA_WIDTH, GLA_WIDTH)
    const = lambda i: (0, 0)
    tile = lambda i: (i, 0)
    out_shape = [jax.ShapeDtypeStruct((tokens, w), BF16) for w in widths]
    out_shape.append(jax.ShapeDtypeStruct((tokens, GLA_KEYS), F32))
    out_specs = [pl.BlockSpec((rows, w), tile) for w in widths]
    out_specs.append(pl.BlockSpec((rows, GLA_KEYS), tile))
    return pl.pallas_call(
        _proj_kernel,
        out_shape=out_shape,
        grid=(tokens // rows,),
        in_specs=[
            pl.BlockSpec((rows, D_MODEL), tile),
            pl.BlockSpec((D_MODEL, MAIN_WIDTH), const),
            pl.BlockSpec((D_MODEL, GLA_GATE_RANK), const),
            pl.BlockSpec((GLA_GATE_RANK, GLA_KEYS), const),
            pl.BlockSpec((1, GLA_KEYS), const),
        ],
        out_specs=out_specs,
        compiler_params=pltpu.CompilerParams(
            dimension_semantics=("parallel",), vmem_limit_bytes=VMEM_LIMIT_BYTES),
        name="input_projection",
    )(x, w_main, w_ga, gate_up, gate_bias)


def _sb_kernel(q_ref, k_ref, v_ref, o_ref, qm_ref, acc_ref, carry_ref):
    qi = pl.program_id(1)
    blk = q_ref.shape[0]
    lane = lax.broadcasted_iota(jnp.int32, (1, LANES), 1)
    head_masks = (lane < SB_HEAD_DIM, lane >= SB_HEAD_DIM)
    row = lax.broadcasted_iota(jnp.int32, (blk, blk), 0)
    col = lax.broadcasted_iota(jnp.int32, (blk, blk), 1)
    strict = col < row
    suffix_mat = (row > col).astype(BF16)

    for h in range(SB_HEADS):
        pair = slice((h // 2) * LANES, (h // 2 + 1) * LANES)
        qm_ref[h] = jnp.where(head_masks[h % 2], q_ref[:, pair].astype(F32), 0.0).astype(BF16)
    acc_ref[...] = jnp.zeros_like(acc_ref)
    carry_ref[...] = jnp.zeros_like(carry_ref)

    def visit_block(key_start, diagonal):
        for hp in range(SB_HEADS // 2):
            pair = slice(hp * LANES, (hp + 1) * LANES)
            kp = k_ref[pl.ds(key_start, blk), pair]
            vp = v_ref[pl.ds(key_start, blk), pair].astype(F32)
            pv = None
            for hh in range(2):
                h = 2 * hp + hh
                z = lax.dot_general(qm_ref[h], kp, NT_DIMS, preferred_element_type=F32)
                sp = _softplus(z)
                log_1m = -sp
                if diagonal:
                    log_1m = jnp.where(strict, log_1m, 0.0)
                suffix = _dot(log_1m.astype(BF16), suffix_mat)
                carry = carry_ref[h]
                carry_wide = jnp.concatenate([carry] * (blk // LANES), axis=1)
                w = jnp.exp((z - sp) + suffix + carry_wide)
                if diagonal:
                    w = jnp.where(strict, w, 0.0)
                vm = jnp.where(head_masks[hh], vp, 0.0).astype(BF16)
                d = _dot(w.astype(BF16), vm)
                pv = d if pv is None else pv + d
                block_total = suffix[:, 0:1] + log_1m[:, 0:1]
                carry_ref[h] = carry + block_total
            acc_ref[:, pair] += pv

    def any_row_alive():
        m = carry_ref[0]
        for h in range(1, SB_HEADS):
            m = jnp.maximum(m, carry_ref[h])
        return (jnp.max(m) > SB_DEAD_LOG_WEIGHT).astype(jnp.int32)

    visit_block(pl.multiple_of(qi * blk, blk), True)

    def cond(state):
        kb, alive = state
        return jnp.logical_and(kb >= 0, alive > 0)

    def body(state):
        kb, _ = state
        visit_block(pl.multiple_of(kb * blk, blk), False)
        return kb - 1, any_row_alive()

    lax.while_loop(cond, body, (qi - 1, any_row_alive()))
    o_ref[...] = acc_ref[...].astype(o_ref.dtype)


def _stick_breaking(q, k, v, batch, seq):
    blk = SB_BLOCK
    n_q = seq // blk
    return pl.pallas_call(
        _sb_kernel,
        out_shape=jax.ShapeDtypeStruct(q.shape, BF16),
        grid=(batch, n_q),
        in_specs=[
            pl.BlockSpec((blk, SB_WIDTH), lambda b, i: (b * n_q + i, 0)),
            pl.BlockSpec((seq, SB_WIDTH), lambda b, i: (b, 0)),
            pl.BlockSpec((seq, SB_WIDTH), lambda b, i: (b, 0)),
        ],
        out_specs=pl.BlockSpec((blk, SB_WIDTH), lambda b, i: (b * n_q + i, 0)),
        scratch_shapes=[
            pltpu.VMEM((SB_HEADS, blk, LANES), BF16),
            pltpu.VMEM((blk, SB_WIDTH), F32),
            pltpu.VMEM((SB_HEADS, blk, LANES), F32),
        ],
        compiler_params=pltpu.CompilerParams(
            dimension_semantics=("parallel", "parallel"), vmem_limit_bytes=VMEM_LIMIT_BYTES),
        name="stick_breaking_attention",
    )(q, k, v)


def _gla_kernel(q_ref, k_ref, v_ref, gg_ref, la_ref, ng_ref, o_ref, st_ref):
    @pl.when(pl.program_id(1) == 0)
    def _():
        st_ref[...] = jnp.zeros_like(st_ref)

    ch = GLA_CHUNK
    r = lax.broadcasted_iota(jnp.int32, (ch, ch), 0)
    c = lax.broadcasted_iota(jnp.int32, (ch, ch), 1)
    causal = c <= r
    cumsum_mat = causal.astype(BF16)
    lane = lax.broadcasted_iota(jnp.int32, (1, LANES), 1)
    head_masks = (lane < GLA_KEY_DIM, lane >= GLA_KEY_DIM)
    norm_g = ng_ref[...]

    for ci in range(q_ref.shape[0] // ch):
        rows = slice(ci * ch, (ci + 1) * ch)
        g_hi, g_lo = _split_bf16(la_ref[rows, :])
        b = _dot(cumsum_mat, g_hi) + _dot(cumsum_mat, g_lo)
        b_mid = b[ch // 2 - 1:ch // 2, :]
        b_last = b[ch - 1:ch, :]
        q = q_ref[rows, :].astype(F32)
        k = k_ref[rows, :].astype(F32)
        q_in = q * jnp.exp(b - b_mid)
        k_in = k * jnp.exp(b_mid - b)
        k_dec = k * jnp.exp(b_last - b)
        q_b = q * jnp.exp(b)
        decay = jnp.exp(b_last)
        for h in range(GLA_HEADS):
            pair = slice((h // 2) * LANES, (h // 2 + 1) * LANES)
            vcols = slice(h * GLA_VAL_DIM, (h + 1) * GLA_VAL_DIM)
            m = head_masks[h % 2]
            qi_m = jnp.where(m, q_in[:, pair], 0.0).astype(BF16)
            scores = lax.dot_general(qi_m, k_in[:, pair].astype(BF16), NT_DIMS,
                                     preferred_element_type=F32)
            scores = jnp.where(causal, scores, 0.0)
            v_h = v_ref[rows, vcols]
            o = _dot(scores.astype(BF16), v_h)
            st = st_ref[h]
            qb_m = jnp.where(m, q_b[:, pair], 0.0).astype(BF16)
            o = o + lax.dot_general(qb_m, st.astype(BF16), NT_DIMS, preferred_element_type=F32)
            kd_m = jnp.where(m, k_dec[:, pair], 0.0).astype(BF16)
            upd = _dot(v_h.astype(F32).T.astype(BF16), kd_m)
            st_ref[h] = st * decay[:, pair] + upd
            on = o * lax.rsqrt(jnp.mean(o * o, axis=-1, keepdims=True) + RMS_EPS) * norm_g
            gate = gg_ref[rows, vcols].astype(F32)
            o_ref[rows, vcols] = (on * (gate / (1.0 + jnp.exp(-gate)))).astype(o_ref.dtype)


def _gla(gq, gk, gv, gg, log_a, norm_g, batch, seq):
    rows = GLA_ROWS
    n = seq // rows
    tile = lambda b, i: (b * n + i, 0)
    return pl.pallas_call(
        _gla_kernel,
        out_shape=jax.ShapeDtypeStruct(gv.shape, BF16),
        grid=(batch, n),
        in_specs=[
            pl.BlockSpec((rows, GLA_KEYS), tile),
            pl.BlockSpec((rows, GLA_KEYS), tile),
            pl.BlockSpec((rows, GLA_WIDTH), tile),
            pl.BlockSpec((rows, GLA_WIDTH), tile),
            pl.BlockSpec((rows, GLA_KEYS), tile),
            pl.BlockSpec((1, GLA_VAL_DIM), lambda b, i: (0, 0)),
        ],
        out_specs=pl.BlockSpec((rows, GLA_WIDTH), tile),
        scratch_shapes=[pltpu.VMEM((GLA_HEADS, GLA_VAL_DIM, LANES), F32)],
        compiler_params=pltpu.CompilerParams(
            dimension_semantics=("parallel", "arbitrary"), vmem_limit_bytes=VMEM_LIMIT_BYTES),
        name="gated_linear_attention",
    )(gq, gk, gv, gg, log_a, norm_g)


def _mix_kernel(alpha, sb_ref, gla_ref, x_ref, w_ref, g_ref, b_ref, h_ref, hb_ref):
    mix = _dot(sb_ref[...], w_ref[0:SB_WIDTH, :]) + _dot(gla_ref[...], w_ref[SB_WIDTH:, :])
    h = _layer_norm(alpha * x_ref[...] + mix, g_ref[...], b_ref[...])
    h_ref[...] = h
    hb_ref[...] = h.astype(BF16)


def _mix_ln(sb_o, gla_o, x, w_out, g, b, alpha):
    tokens = x.shape[0]
    rows = MIX_ROWS
    tile = lambda i: (i, 0)
    const = lambda i: (0, 0)
    return pl.pallas_call(
        functools.partial(_mix_kernel, alpha),
        out_shape=[jax.ShapeDtypeStruct((tokens, D_MODEL), F32),
                   jax.ShapeDtypeStruct((tokens, D_MODEL), BF16)],
        grid=(tokens // rows,),
        in_specs=[
            pl.BlockSpec((rows, SB_WIDTH), tile),
            pl.BlockSpec((rows, GLA_WIDTH), tile),
            pl.BlockSpec((rows, D_MODEL), tile),
            pl.BlockSpec((D_MODEL, D_MODEL), const),
            pl.BlockSpec((1, D_MODEL), const),
            pl.BlockSpec((1, D_MODEL), const),
        ],
        out_specs=[pl.BlockSpec((rows, D_MODEL), tile), pl.BlockSpec((rows, D_MODEL), tile)],
        compiler_params=pltpu.CompilerParams(
            dimension_semantics=("parallel",), vmem_limit_bytes=VMEM_LIMIT_BYTES),
        name="output_projection_ln",
    )(sb_o, gla_o, x, w_out, g, b)


def _ffn_kernel(alpha, tiles_per_seq, hb_ref, halo_ref, h_ref, wa_ref, wc_ref, cwa_ref, cwc_ref,
                cba_ref, cbc_ref, wd_ref, g_ref, b_ref, o_ref, ua_ref, uc_ref, act_ref, acc_ref):
    i = pl.program_id(0)
    j = pl.program_id(1)
    rows = hb_ref.shape[0]
    halo = halo_ref[...]
    first_of_seq = (i % tiles_per_seq) == 0
    halo = jnp.where(first_of_seq, jnp.zeros_like(halo), halo)
    hx = jnp.concatenate([halo, hb_ref[...]], axis=0)

    def conv(u_ref, w_ref, b_ref, c0, cw):
        w = w_ref[:, c0:c0 + cw]
        return (w[2:3, :] * u_ref[CONV_HALO:CONV_HALO + rows, 0:cw]
                + w[1:2, :] * u_ref[CONV_HALO - 1:CONV_HALO - 1 + rows, 0:cw]
                + w[0:1, :] * u_ref[CONV_HALO - 2:CONV_HALO - 2 + rows, 0:cw]
                + b_ref[:, c0:c0 + cw])

    for c0, cw in FFN_CHUNKS:
        ua_ref[:, 0:cw] = _dot(hx, wa_ref[:, c0:c0 + cw])
        uc_ref[:, 0:cw] = _dot(hx, wc_ref[:, c0:c0 + cw])
        a = conv(ua_ref, cwa_ref, cba_ref, c0, cw)
        c = conv(uc_ref, cwc_ref, cbc_ref, c0, cw)
        gelu = 0.5 * a * (1.0 + lax.erf(a * (2.0 ** -0.5)))
        act_ref[:, c0:c0 + cw] = (gelu * c).astype(BF16)

    part = _dot(act_ref[...], wd_ref[...])

    @pl.when(j == 0)
    def _():
        acc_ref[...] = part

    @pl.when(j == pl.num_programs(1) - 1)
    def _():
        f = acc_ref[...] + part
        o_ref[...] = _layer_norm(alpha * h_ref[...] + f, g_ref[...], b_ref[...])


def _ffn_ln(hb, h, w_up, conv_w, conv_b, w_down, g, b, alpha, seq):
    tokens = h.shape[0]
    rows = FFN_ROWS
    cols = FFN_COLS
    n_col = D_FF // cols
    assert n_col == 2 and sum(cw for _, cw in FFN_CHUNKS) == cols
    halo_blocks = rows // CONV_HALO
    tile = lambda i, j: (i, 0)
    const = lambda i, j: (0, 0)
    a_cols = lambda i, j: (0, j)
    c_cols = lambda i, j: (0, n_col + j)
    return pl.pallas_call(
        functools.partial(_ffn_kernel, alpha, seq // rows),
        out_shape=jax.ShapeDtypeStruct((tokens, D_MODEL), F32),
        grid=(tokens // rows, n_col),
        in_specs=[
            pl.BlockSpec((rows, D_MODEL), tile),
            pl.BlockSpec((CONV_HALO, D_MODEL), lambda i, j: (jnp.maximum(i * halo_blocks - 1, 0), 0)),
            pl.BlockSpec((rows, D_MODEL), tile),
            pl.BlockSpec((D_MODEL, cols), a_cols),
            pl.BlockSpec((D_MODEL, cols), c_cols),
            pl.BlockSpec((CONV_WIDTH, cols), a_cols),
            pl.BlockSpec((CONV_WIDTH, cols), c_cols),
            pl.BlockSpec((1, cols), a_cols),
            pl.BlockSpec((1, cols), c_cols),
            pl.BlockSpec((cols, D_MODEL), lambda i, j: (j, 0)),
            pl.BlockSpec((1, D_MODEL), const),
            pl.BlockSpec((1, D_MODEL), const),
        ],
        out_specs=pl.BlockSpec((rows, D_MODEL), tile),
        scratch_shapes=[
            pltpu.VMEM((CONV_HALO + rows, 256), F32),
            pltpu.VMEM((CONV_HALO + rows, 256), F32),
            pltpu.VMEM((rows, cols), BF16),
            pltpu.VMEM((rows, D_MODEL), F32),
        ],
        compiler_params=pltpu.CompilerParams(
            dimension_semantics=("parallel", "arbitrary"), vmem_limit_bytes=VMEM_LIMIT_BYTES),
        name="conv_ffn_ln",
    )(hb, hb, h, w_up, w_up, conv_w, conv_w, conv_b, conv_b, w_down, g, b)


def _layer(x, w_in, gate_up, gate_bias, gla_norm_g, w_out, ln1_g, ln1_b,
           w_up, conv_w, conv_b, w_down, ln2_g, ln2_b, alpha, batch, seq):
    row = lambda t: t.reshape(1, -1)
    q, k, v, gq, gk, gv, gg, log_a = _input_projection(
        x, w_in[:, :MAIN_WIDTH].astype(BF16), w_in[:, MAIN_WIDTH:], gate_up, row(gate_bias))
    sb_o = _stick_breaking(q, k, v, batch, seq)
    gla_o = _gla(gq, gk, gv, gg, log_a, row(gla_norm_g), batch, seq)
    h, hb = _mix_ln(sb_o, gla_o, x, w_out.astype(BF16), row(ln1_g), row(ln1_b), alpha)
    return _ffn_ln(hb, h, w_up.astype(BF16), conv_w, row(conv_b), w_down.astype(BF16),
                   row(ln2_g), row(ln2_b), alpha, seq)


def kernel(x, w_in, gate_up, gate_bias, gla_norm_g, w_out, ln1_g, ln1_b, w_up, conv_w, conv_b,
           w_down, ln2_g, ln2_b):
    batch, seq, d_model = x.shape
    depth = w_in.shape[0]
    alpha = (2.0 * depth) ** 0.25
    y = x.reshape(batch * seq, d_model)
    for l in range(depth):
        y = _layer(y, w_in[l], gate_up[l], gate_bias[l], gla_norm_g[l], w_out[l], ln1_g[l], ln1_b[l],
                   w_up[l], conv_w[l], conv_b[l], w_down[l], ln2_g[l], ln2_b[l], alpha, batch, seq)
    return y.reshape(batch, seq, d_model)
```

```python
import functools

import jax
import jax.numpy as jnp
from jax import lax
from jax.experimental import pallas as pl
from jax.experimental.pallas import tpu as pltpu

F32 = jnp.float32
BF16 = jnp.bfloat16

D_MODEL = 1024
SB_WIDTH = 512
SB_HEADS = 8
SB_HEAD_DIM = 64
GLA_WIDTH = 512
GLA_HEADS = 4
GLA_KEY_DIM = 64
GLA_VAL_DIM = 128
GLA_KEYS = GLA_HEADS * GLA_KEY_DIM
GLA_GATE_RANK = 16
GLA_TAU = 16.0
GLA_CHUNK = 64
D_FF = 2816
CONV_WIDTH = 3
LN_EPS = 1e-5
RMS_EPS = 1e-6
MAIN_WIDTH = 3 * SB_WIDTH + 2 * GLA_KEYS + 2 * GLA_WIDTH

LANES = 128
SUBLANES = 8
MXU_WIDTH = 256
VMEM_LIMIT_BYTES = 56 * 1024 * 1024

PROJ_ROWS = 512
SB_Q_ROWS = 128
SB_BACK_KEYS = 256
GLA_ROWS = 128
FFN_ROWS = 512
FFN_CHUNK = MXU_WIDTH
CONV_TAIL = SUBLANES

SB_DEAD_LOG_WEIGHT = -90.0

NT_DIMS = (((1,), (1,)), ((), ()))


def _split_bf16(x):
    hi = x.astype(BF16)
    lo = (x - hi.astype(F32)).astype(BF16)
    return hi, lo


def _dot(a, b):
    return jnp.dot(a, b, preferred_element_type=F32)


def _dot_nt(a, b):
    return lax.dot_general(a, b, NT_DIMS, preferred_element_type=F32)


def _softplus(z):
    return jnp.maximum(z, 0.0) + jnp.log(1.0 + jnp.exp(-jnp.abs(z)))


def _layer_norm(y, g, b):
    mu = jnp.mean(y, axis=-1, keepdims=True)
    d = y - mu
    var = jnp.mean(d * d, axis=-1, keepdims=True)
    return d * lax.rsqrt(var + LN_EPS) * g + b


def _proj_kernel(x_ref, w_ref, wga_ref, gup_ref, gbias_ref,
                 q_ref, k_ref, v_ref, gq_ref, gk_ref, gv_ref, gg_ref, la_ref):
    xb = x_ref[...].astype(BF16)
    col = 0
    for out_ref, scale in ((q_ref, SB_HEAD_DIM ** -0.5), (k_ref, None), (v_ref, None),
                           (gq_ref, GLA_KEY_DIM ** -0.5), (gk_ref, None), (gv_ref, None),
                           (gg_ref, None)):
        width = out_ref.shape[1]
        y = _dot(xb, w_ref[:, col:col + width])
        if scale is not None:
            y = y * scale
        out_ref[...] = y.astype(out_ref.dtype)
        col += width
    ga = _dot(xb, wga_ref[...])
    logits = _dot(ga.astype(BF16), gup_ref[...].astype(BF16)) + gbias_ref[...]
    la_ref[...] = -_softplus(-logits) * (1.0 / GLA_TAU)


def _input_projection(x, w_main, w_ga, gate_up, gate_bias):
    tokens = x.shape[0]
    rows = PROJ_ROWS
    widths = (SB_WIDTH, SB_WIDTH, SB_WIDTH, GLA_KEYS, GLA_KEYS, GLA_WIDTH, GLA_WIDTH)
    const = lambda i: (0, 0)
    tile = lambda i: (i, 0)
    out_shape = [jax.ShapeDtypeStruct((tokens, w), BF16) for w in widths]
    out_shape.append(jax.ShapeDtypeStruct((tokens, GLA_KEYS), F32))
    out_specs = [pl.BlockSpec((rows, w), tile) for w in widths]
    out_specs.append(pl.BlockSpec((rows, GLA_KEYS), tile))
    return pl.pallas_call(
        _proj_kernel,
        out_shape=out_shape,
        grid=(tokens // rows,),
        in_specs=[
            pl.BlockSpec((rows, D_MODEL), tile),
            pl.BlockSpec((D_MODEL, MAIN_WIDTH), const),
            pl.BlockSpec((D_MODEL, GLA_GATE_RANK), const),
            pl.BlockSpec((GLA_GATE_RANK, GLA_KEYS), const),
            pl.BlockSpec((1, GLA_KEYS), const),
        ],
        out_specs=out_specs,
        compiler_params=pltpu.CompilerParams(
            dimension_semantics=("parallel",), vmem_limit_bytes=VMEM_LIMIT_BYTES),
        name="input_projection",
    )(x, w_main, w_ga, gate_up, gate_bias)


def _sb_kernel(q_ref, k_ref, v_ref, o_ref, qm_ref, acc_ref, carry_ref, lsig_ref, sp_ref):
    qi = pl.program_id(1)
    tq = q_ref.shape[0]
    wide = SB_BACK_KEYS
    n_pairs = SB_HEADS // 2
    lane = lax.broadcasted_iota(jnp.int32, (1, LANES), 1)
    even_lanes = lane < SB_HEAD_DIM

    def neg_suffix_mat(width):
        row = lax.broadcasted_iota(jnp.int32, (width, width), 0)
        col = lax.broadcasted_iota(jnp.int32, (width, width), 1)
        return jnp.where(row > col, -1.0, 0.0).astype(BF16)

    suffix_mats = {tq: neg_suffix_mat(tq), wide: neg_suffix_mat(wide)}

    for p in range(n_pairs):
        qp = q_ref[:, p * LANES:(p + 1) * LANES].astype(F32)
        qm_ref[p, 0:tq, :] = jnp.where(even_lanes, qp, 0.0).astype(BF16)
        qm_ref[p, tq:2 * tq, :] = jnp.where(even_lanes, 0.0, qp).astype(BF16)
    acc_ref[...] = jnp.zeros_like(acc_ref)
    carry_ref[...] = jnp.zeros_like(carry_ref)

    def visit_blocks(blocks):
        col0 = 0
        placed = []
        for key_start, width, diagonal in blocks:
            cols = slice(col0, col0 + width)
            strict = None
            if diagonal:
                r2 = lax.broadcasted_iota(jnp.int32, (2 * tq, width), 0)
                c2 = lax.broadcasted_iota(jnp.int32, (2 * tq, width), 1)
                strict = c2 < jnp.bitwise_and(r2, tq - 1)
            for p in range(n_pairs):
                kp = k_ref[pl.ds(key_start, width), p * LANES:(p + 1) * LANES]
                z = _dot_nt(qm_ref[p], kp)
                sp = _softplus(z)
                lsig_ref[p, :, cols] = z - sp
                if diagonal:
                    sp = jnp.where(strict, sp, 0.0)
                sp_ref[p, :, cols] = sp.astype(BF16)
            placed.append((key_start, width, strict, cols))
            col0 += width
        for key_start, width, strict, cols in placed:
            for p in range(n_pairs):
                sp_b = sp_ref[p, :, cols]
                suffix = _dot(sp_b, suffix_mats[width])
                carry = carry_ref[p]
                carry_wide = jnp.concatenate([carry] * (width // LANES), axis=1)
                w = jnp.exp(lsig_ref[p, :, cols] + suffix + carry_wide)
                if strict is not None:
                    w = jnp.where(strict, w, 0.0)
                vp = v_ref[pl.ds(key_start, width), p * LANES:(p + 1) * LANES]
                d = _dot(w.astype(BF16), vp)
                acc_ref[:, p * LANES:(p + 1) * LANES] += jnp.where(even_lanes, d[0:tq], d[tq:2 * tq])
                block_total = suffix[:, 0:1] - sp_b[:, 0:1].astype(F32)
                carry_ref[p] = carry + block_total

    def any_row_alive():
        m = carry_ref[0]
        for p in range(1, n_pairs):
            m = jnp.maximum(m, carry_ref[p])
        return (jnp.max(m) > SB_DEAD_LOG_WEIGHT).astype(jnp.int32)

    q0 = pl.multiple_of(qi * tq, tq)
    has_wide = qi >= wide // tq

    @pl.when(has_wide)
    def _():
        visit_blocks([(q0, tq, True), (pl.multiple_of(q0 - wide, tq), wide, False)])

    @pl.when(jnp.logical_not(has_wide))
    def _():
        visit_blocks([(q0, tq, True)])

    def cond(state):
        kb, alive = state
        return jnp.logical_and(kb >= 0, alive > 0)

    def body(state):
        kb, _ = state
        visit_blocks([(pl.multiple_of(kb * tq, tq), tq, False)])
        return kb - 1, any_row_alive()

    next_block = jnp.where(has_wide, qi - wide // tq, qi) - 1
    lax.while_loop(cond, body, (next_block, any_row_alive()))
    o_ref[...] = acc_ref[...].astype(o_ref.dtype)


def _stick_breaking(q, k, v, batch, seq):
    tq = SB_Q_ROWS
    n_q = seq // tq
    n_pairs = SB_HEADS // 2
    return pl.pallas_call(
        _sb_kernel,
        out_shape=jax.ShapeDtypeStruct(q.shape, BF16),
        grid=(batch, n_q),
        in_specs=[
            pl.BlockSpec((tq, SB_WIDTH), lambda b, i: (b * n_q + i, 0)),
            pl.BlockSpec((seq, SB_WIDTH), lambda b, i: (b, 0)),
            pl.BlockSpec((seq, SB_WIDTH), lambda b, i: (b, 0)),
        ],
        out_specs=pl.BlockSpec((tq, SB_WIDTH), lambda b, i: (b * n_q + i, 0)),
        scratch_shapes=[
            pltpu.VMEM((n_pairs, 2 * tq, LANES), BF16),
            pltpu.VMEM((tq, SB_WIDTH), F32),
            pltpu.VMEM((n_pairs, 2 * tq, LANES), F32),
            pltpu.VMEM((n_pairs, 2 * tq, tq + SB_BACK_KEYS), F32),
            pltpu.VMEM((n_pairs, 2 * tq, tq + SB_BACK_KEYS), BF16),
        ],
        compiler_params=pltpu.CompilerParams(
            dimension_semantics=("parallel", "parallel"), vmem_limit_bytes=VMEM_LIMIT_BYTES),
        name="stick_breaking_attention",
    )(q, k, v)


def _gla_kernel(q_ref, k_ref, v_ref, gg_ref, la_ref, ng_ref, o_ref, st_ref):
    @pl.when(pl.program_id(0) == 0)
    def _():
        st_ref[...] = jnp.zeros_like(st_ref)

    ch = GLA_CHUNK
    n_batch = q_ref.shape[0]
    r = lax.broadcasted_iota(jnp.int32, (ch, ch), 0)
    c = lax.broadcasted_iota(jnp.int32, (ch, ch), 1)
    cumsum_mat = (c <= r).astype(BF16)
    r2 = lax.broadcasted_iota(jnp.int32, (2 * ch, 2 * ch), 0)
    c2 = lax.broadcasted_iota(jnp.int32, (2 * ch, 2 * ch), 1)
    same_head_causal = jnp.logical_and(jnp.bitwise_and(r2, ch) == jnp.bitwise_and(c2, ch),
                                       jnp.bitwise_and(c2, ch - 1) <= jnp.bitwise_and(r2, ch - 1))
    lane = lax.broadcasted_iota(jnp.int32, (1, LANES), 1)
    even_lanes = lane < GLA_KEY_DIM
    norm_g = ng_ref[...]

    def stack_masked(t):
        return jnp.concatenate([jnp.where(even_lanes, t, 0.0), jnp.where(even_lanes, 0.0, t)],
                               axis=0).astype(BF16)

    for ci in range(q_ref.shape[1] // ch):
        rows = slice(ci * ch, (ci + 1) * ch)
        for bi in range(n_batch):
            g_hi, g_lo = _split_bf16(la_ref[bi, rows, :])
            b = _dot(cumsum_mat, g_hi) + _dot(cumsum_mat, g_lo)
            b_mid = b[ch // 2 - 1:ch // 2, :]
            b_last = b[ch - 1:ch, :]
            q = q_ref[bi, rows, :].astype(F32)
            k = k_ref[bi, rows, :].astype(F32)
            q_in = q * jnp.exp(b - b_mid)
            k_in = (k * jnp.exp(b_mid - b)).astype(BF16)
            k_dec = k * jnp.exp(b_last - b)
            q_b = q * jnp.exp(b)
            decay = jnp.exp(b_last)
            for p in range(GLA_HEADS // 2):
                pair = slice(p * LANES, (p + 1) * LANES)
                vcols = slice(2 * p * GLA_VAL_DIM, (2 * p + 2) * GLA_VAL_DIM)
                v_pair = v_ref[bi, rows, vcols]
                v_st = jnp.concatenate([v_pair[:, :GLA_VAL_DIM], v_pair[:, GLA_VAL_DIM:]], axis=0)
                k_pair = k_in[:, pair]
                scores = _dot_nt(stack_masked(q_in[:, pair]), jnp.concatenate([k_pair, k_pair], axis=0))
                scores = jnp.where(same_head_causal, scores, 0.0).astype(BF16)
                st = st_ref[bi, p]
                o = _dot(scores, v_st) + _dot_nt(stack_masked(q_b[:, pair]), st.astype(BF16))
                upd = _dot(v_st.astype(F32).T.astype(BF16), stack_masked(k_dec[:, pair]))
                st_ref[bi, p] = st * decay[:, pair] + upd
                on = o * lax.rsqrt(jnp.mean(o * o, axis=-1, keepdims=True) + RMS_EPS) * norm_g
                gate = gg_ref[bi, rows, vcols].astype(F32)
                gate = jnp.concatenate([gate[:, :GLA_VAL_DIM], gate[:, GLA_VAL_DIM:]], axis=0)
                res = (on * (gate / (1.0 + jnp.exp(-gate)))).astype(o_ref.dtype)
                o_ref[bi, rows, vcols] = jnp.concatenate([res[:ch], res[ch:]], axis=1)


def _gla(gq, gk, gv, gg, log_a, norm_g, batch, seq):
    rows = GLA_ROWS
    shaped = lambda t: t.reshape(batch, seq, t.shape[-1])
    tile = lambda i: (0, i, 0)
    out = pl.pallas_call(
        _gla_kernel,
        out_shape=jax.ShapeDtypeStruct((batch, seq, GLA_WIDTH), BF16),
        grid=(seq // rows,),
        in_specs=[
            pl.BlockSpec((batch, rows, GLA_KEYS), tile),
            pl.BlockSpec((batch, rows, GLA_KEYS), tile),
            pl.BlockSpec((batch, rows, GLA_WIDTH), tile),
            pl.BlockSpec((batch, rows, GLA_WIDTH), tile),
            pl.BlockSpec((batch, rows, GLA_KEYS), tile),
            pl.BlockSpec((1, GLA_VAL_DIM), lambda i: (0, 0)),
        ],
        out_specs=pl.BlockSpec((batch, rows, GLA_WIDTH), tile),
        scratch_shapes=[pltpu.VMEM((batch, GLA_HEADS // 2, GLA_VAL_DIM, LANES), F32)],
        compiler_params=pltpu.CompilerParams(
            dimension_semantics=("arbitrary",), vmem_limit_bytes=VMEM_LIMIT_BYTES),
        name="gated_linear_attention",
    )(shaped(gq), shaped(gk), shaped(gv), shaped(gg), shaped(log_a), norm_g)
    return out.reshape(batch * seq, GLA_WIDTH)


def _mix_ffn_kernel(alpha, tiles_per_seq, sb_ref, gla_ref, x_ref, wo_ref, g1_ref, b1_ref,
                    wup_ref, cw_ref, cb_ref, wd_ref, g2_ref, b2_ref, o_ref,
                    h_ref, ua_ref, uc_ref, tail_a_ref, tail_c_ref, act_ref):
    rows = x_ref.shape[0]
    n_chunks = D_FF // FFN_CHUNK

    @pl.when(pl.program_id(0) % tiles_per_seq == 0)
    def _():
        tail_a_ref[...] = jnp.zeros_like(tail_a_ref)
        tail_c_ref[...] = jnp.zeros_like(tail_c_ref)

    mix = _dot(sb_ref[...], wo_ref[0:SB_WIDTH, :]) + _dot(gla_ref[...], wo_ref[SB_WIDTH:, :])
    h = _layer_norm(alpha * x_ref[...] + mix, g1_ref[...], b1_ref[...])
    h_ref[...] = h
    hb = h.astype(BF16)

    def conv(u_ref, slot, c0):
        w = cw_ref[:, c0:c0 + FFN_CHUNK]
        return (w[2:3, :] * u_ref[slot, CONV_TAIL:CONV_TAIL + rows, :]
                + w[1:2, :] * u_ref[slot, CONV_TAIL - 1:CONV_TAIL - 1 + rows, :]
                + w[0:1, :] * u_ref[slot, CONV_TAIL - 2:CONV_TAIL - 2 + rows, :]
                + cb_ref[:, c0:c0 + FFN_CHUNK])

    for ci in range(n_chunks):
        slot = ci % 2
        a0 = ci * FFN_CHUNK
        c0 = D_FF + ci * FFN_CHUNK
        for u_ref, tail_ref, col in ((ua_ref, tail_a_ref, a0), (uc_ref, tail_c_ref, c0)):
            u_ref[slot, 0:CONV_TAIL, :] = tail_ref[ci]
            u_ref[slot, CONV_TAIL:, :] = _dot(hb, wup_ref[:, col:col + FFN_CHUNK])
            tail_ref[ci] = u_ref[slot, rows:rows + CONV_TAIL, :]
        a = conv(ua_ref, slot, a0)
        c = conv(uc_ref, slot, c0)
        act_ref[:, a0:a0 + FFN_CHUNK] = (0.5 * a * (1.0 + lax.erf(a * (2.0 ** -0.5))) * c).astype(BF16)

    f = _dot(act_ref[...], wd_ref[...])
    o_ref[...] = _layer_norm(alpha * h_ref[...] + f, g2_ref[...], b2_ref[...])


def _mix_ffn(sb_o, gla_o, x, w_out, g1, b1, w_up, conv_w, conv_b, w_down, g2, b2, alpha, seq):
    tokens = x.shape[0]
    rows = FFN_ROWS
    n_chunks = D_FF // FFN_CHUNK
    assert n_chunks * FFN_CHUNK == D_FF and seq % rows == 0
    tile = lambda i: (i, 0)
    const = lambda i: (0, 0)
    whole = lambda t: pl.BlockSpec(t.shape, const)
    return pl.pallas_call(
        functools.partial(_mix_ffn_kernel, alpha, seq // rows),
        out_shape=jax.ShapeDtypeStruct((tokens, D_MODEL), F32),
        grid=(tokens // rows,),
        in_specs=[
            pl.BlockSpec((rows, SB_WIDTH), tile),
            pl.BlockSpec((rows, GLA_WIDTH), tile),
            pl.BlockSpec((rows, D_MODEL), tile),
            whole(w_out), whole(g1), whole(b1),
            whole(w_up), whole(conv_w), whole(conv_b), whole(w_down), whole(g2), whole(b2),
        ],
        out_specs=pl.BlockSpec((rows, D_MODEL), tile),
        scratch_shapes=[
            pltpu.VMEM((rows, D_MODEL), F32),
            pltpu.VMEM((2, CONV_TAIL + rows, FFN_CHUNK), F32),
            pltpu.VMEM((2, CONV_TAIL + rows, FFN_CHUNK), F32),
            pltpu.VMEM((n_chunks, CONV_TAIL, FFN_CHUNK), F32),
            pltpu.VMEM((n_chunks, CONV_TAIL, FFN_CHUNK), F32),
            pltpu.VMEM((rows, D_FF), BF16),
        ],
        compiler_params=pltpu.CompilerParams(
            dimension_semantics=("arbitrary",), vmem_limit_bytes=VMEM_LIMIT_BYTES),
        name="mix_ffn",
    )(sb_o, gla_o, x, w_out, g1, b1, w_up, conv_w, conv_b, w_down, g2, b2)


def _layer(x, w_in, gate_up, gate_bias, gla_norm_g, w_out, ln1_g, ln1_b,
           w_up, conv_w, conv_b, w_down, ln2_g, ln2_b, alpha, batch, seq):
    row = lambda t: t.reshape(1, -1)
    w_in_b = w_in.astype(BF16)
    q, k, v, gq, gk, gv, gg, log_a = _input_projection(
        x, w_in_b[:, :MAIN_WIDTH], w_in_b[:, MAIN_WIDTH:], gate_up, row(gate_bias))
    sb_o = _stick_breaking(q, k, v, batch, seq)
    gla_o = _gla(gq, gk, gv, gg, log_a, row(gla_norm_g), batch, seq)
    return _mix_ffn(sb_o, gla_o, x, w_out.astype(BF16), row(ln1_g), row(ln1_b),
                    w_up.astype(BF16), conv_w, row(conv_b), w_down.astype(BF16),
                    row(ln2_g), row(ln2_b), alpha, seq)


def kernel(x, w_in, gate_up, gate_bias, gla_norm_g, w_out, ln1_g, ln1_b, w_up, conv_w, conv_b,
           w_down, ln2_g, ln2_b):
    batch, seq, d_model = x.shape
    depth = w_in.shape[0]
    alpha = (2.0 * depth) ** 0.25
    y = x.reshape(batch * seq, d_model)
    for l in range(depth):
        y = _layer(y, w_in[l], gate_up[l], gate_bias[l], gla_norm_g[l], w_out[l], ln1_g[l], ln1_b[l],
                   w_up[l], conv_w[l], conv_b[l], w_down[l], ln2_g[l], ln2_b[l], alpha, batch, seq)
    return y.reshape(batch, seq, d_model)
```

```python
import functools

import jax
import jax.numpy as jnp
from jax import lax
from jax.experimental import pallas as pl
from jax.experimental.pallas import tpu as pltpu

F32 = jnp.float32
BF16 = jnp.bfloat16

D_MODEL = 1024
SB_WIDTH = 512
SB_HEADS = 8
SB_HEAD_DIM = 64
GLA_WIDTH = 512
GLA_HEADS = 4
GLA_KEY_DIM = 64
GLA_VAL_DIM = 128
GLA_KEYS = GLA_HEADS * GLA_KEY_DIM
GLA_GATE_RANK = 16
GLA_TAU = 16.0
GLA_CHUNK = 64
D_FF = 2816
CONV_WIDTH = 3
LN_EPS = 1e-5
RMS_EPS = 1e-6
MAIN_WIDTH = 3 * SB_WIDTH + 2 * GLA_KEYS + 2 * GLA_WIDTH

LANES = 128
SUBLANES = 8
MXU_WIDTH = 256
VMEM_LIMIT_BYTES = 56 * 1024 * 1024

PROJ_ROWS = 1024
SB_Q_ROWS = 128
SB_BACK_KEYS = 256
GLA_ROWS = 256
FFN_ROWS = 512
FFN_CHUNK = MXU_WIDTH
CONV_TAIL = SUBLANES

SB_DEAD_LOG_WEIGHT = -90.0
SB_MASKED_LOG_WEIGHT = -1e30
F32_SIGN_BIT = -2 ** 31

NT_DIMS = (((1,), (1,)), ((), ()))


def _split_bf16(x):
    hi = x.astype(BF16)
    lo = (x - hi.astype(F32)).astype(BF16)
    return hi, lo


def _dot(a, b):
    return jnp.dot(a, b, preferred_element_type=F32)


def _dot_nt(a, b):
    return lax.dot_general(a, b, NT_DIMS, preferred_element_type=F32)


def _softplus(z):
    neg_abs = lax.bitcast_convert_type(
        lax.bitcast_convert_type(z, jnp.int32) | jnp.int32(F32_SIGN_BIT), F32)
    return jnp.maximum(z, 0.0) + jnp.log(1.0 + jnp.exp(neg_abs))


def _layer_norm(y, g, b):
    mu = jnp.mean(y, axis=-1, keepdims=True)
    d = y - mu
    var = jnp.mean(d * d, axis=-1, keepdims=True)
    return d * lax.rsqrt(var + LN_EPS) * g + b


def _proj_kernel(x_ref, w_ref, gup_ref, gbias_ref,
                 q_ref, k_ref, v_ref, gq_ref, gk_ref, gv_ref, gg_ref, la_ref):
    xb = x_ref[...].astype(BF16)
    col = 0
    for out_ref, scale in ((q_ref, SB_HEAD_DIM ** -0.5), (k_ref, None), (v_ref, None),
                           (gq_ref, GLA_KEY_DIM ** -0.5), (gk_ref, None), (gv_ref, None),
                           (gg_ref, None)):
        width = out_ref.shape[1]
        y = _dot(xb, w_ref[:, col:col + width])
        if scale is not None:
            y = y * scale
        out_ref[...] = y.astype(out_ref.dtype)
        col += width
    ga = _dot(xb, w_ref[:, col:col + GLA_GATE_RANK])
    logits = _dot(ga.astype(BF16), gup_ref[...].astype(BF16)) + gbias_ref[...]
    la_ref[...] = -_softplus(-logits) * (1.0 / GLA_TAU)


def _input_projection(x, w_in, gate_up, gate_bias):
    tokens = x.shape[0]
    rows = PROJ_ROWS
    widths = (SB_WIDTH, SB_WIDTH, SB_WIDTH, GLA_KEYS, GLA_KEYS, GLA_WIDTH, GLA_WIDTH)
    const = lambda i: (0, 0)
    tile = lambda i: (i, 0)
    out_shape = [jax.ShapeDtypeStruct((tokens, w), BF16) for w in widths]
    out_shape.append(jax.ShapeDtypeStruct((tokens, GLA_KEYS), F32))
    out_specs = [pl.BlockSpec((rows, w), tile) for w in widths]
    out_specs.append(pl.BlockSpec((rows, GLA_KEYS), tile))
    return pl.pallas_call(
        _proj_kernel,
        out_shape=out_shape,
        grid=(tokens // rows,),
        in_specs=[
            pl.BlockSpec((rows, D_MODEL), tile),
            pl.BlockSpec((D_MODEL, MAIN_WIDTH + GLA_GATE_RANK), const),
            pl.BlockSpec((GLA_GATE_RANK, GLA_KEYS), const),
            pl.BlockSpec((1, GLA_KEYS), const),
        ],
        out_specs=out_specs,
        compiler_params=pltpu.CompilerParams(
            dimension_semantics=("parallel",), vmem_limit_bytes=VMEM_LIMIT_BYTES),
        name="input_projection",
    )(x, w_in, gate_up, gate_bias)


def _sb_kernel(q_ref, k_ref, v_ref, o_ref, qm_ref, acc_ref, carry_ref, z_ref, sp_ref):
    qi = pl.program_id(1)
    tq = q_ref.shape[0]
    wide = SB_BACK_KEYS
    n_pairs = SB_HEADS // 2
    lane = lax.broadcasted_iota(jnp.int32, (1, LANES), 1)
    even_lanes = lane < SB_HEAD_DIM

    def neg_suffix_mat(width):
        row = lax.broadcasted_iota(jnp.int32, (width, width), 0)
        col = lax.broadcasted_iota(jnp.int32, (width, width), 1)
        return jnp.where(row >= col, -1.0, 0.0).astype(BF16)

    suffix_mats = {tq: neg_suffix_mat(tq), wide: neg_suffix_mat(wide)}

    for p in range(n_pairs):
        qp = q_ref[:, p * LANES:(p + 1) * LANES].astype(F32)
        qm_ref[p, 0:tq, :] = jnp.where(even_lanes, qp, 0.0).astype(BF16)
        qm_ref[p, tq:2 * tq, :] = jnp.where(even_lanes, 0.0, qp).astype(BF16)

    def visit_blocks(blocks, fresh):
        col0 = 0
        placed = []
        for key_start, width, diagonal in blocks:
            cols = slice(col0, col0 + width)
            strict = None
            if diagonal:
                r2 = lax.broadcasted_iota(jnp.int32, (2 * tq, width), 0)
                c2 = lax.broadcasted_iota(jnp.int32, (2 * tq, width), 1)
                strict = c2 < jnp.bitwise_and(r2, tq - 1)
            for p in range(n_pairs):
                kp = k_ref[pl.ds(key_start, width), p * LANES:(p + 1) * LANES]
                z = _dot_nt(qm_ref[p], kp)
                sp = _softplus(z)
                z_ref[p, :, cols] = z
                if diagonal:
                    sp = jnp.where(strict, sp, 0.0)
                sp_ref[p, :, cols] = sp.astype(BF16)
            placed.append((key_start, width, strict, cols))
            col0 += width
        for bi, (key_start, width, strict, cols) in enumerate(placed):
            first = fresh and bi == 0
            for p in range(n_pairs):
                pcols = slice(p * LANES, (p + 1) * LANES)
                incl = _dot(sp_ref[p, :, cols], suffix_mats[width])
                log_w = z_ref[p, :, cols] + incl
                if not first:
                    carry = carry_ref[p]
                    log_w = log_w + jnp.concatenate([carry] * (width // LANES), axis=1)
                if strict is not None:
                    log_w = jnp.where(strict, log_w, SB_MASKED_LOG_WEIGHT)
                w = jnp.exp(log_w).astype(BF16)
                d = _dot(w, v_ref[pl.ds(key_start, width), pcols])
                pv = jnp.where(even_lanes, d[0:tq], d[tq:2 * tq])
                block_total = incl[:, 0:1]
                if first:
                    acc_ref[:, pcols] = pv
                    carry_ref[p] = jnp.broadcast_to(block_total, (2 * tq, LANES))
                else:
                    acc_ref[:, pcols] += pv
                    carry_ref[p] = carry + block_total

    def any_row_alive():
        m = carry_ref[0]
        for p in range(1, n_pairs):
            m = jnp.maximum(m, carry_ref[p])
        return (jnp.max(m) > SB_DEAD_LOG_WEIGHT).astype(jnp.int32)

    q0 = pl.multiple_of(qi * tq, tq)
    has_wide = qi >= wide // tq

    @pl.when(has_wide)
    def _():
        visit_blocks([(q0, tq, True), (pl.multiple_of(q0 - wide, tq), wide, False)], fresh=True)

    @pl.when(jnp.logical_not(has_wide))
    def _():
        visit_blocks([(q0, tq, True)], fresh=True)

    def cond(state):
        kb, alive = state
        return jnp.logical_and(kb >= 0, alive > 0)

    def body(state):
        kb, _ = state
        visit_blocks([(pl.multiple_of(kb * tq, tq), tq, False)], fresh=False)
        return kb - 1, any_row_alive()

    next_block = jnp.where(has_wide, qi - wide // tq, qi) - 1
    lax.while_loop(cond, body, (next_block, any_row_alive()))
    o_ref[...] = acc_ref[...].astype(o_ref.dtype)


def _stick_breaking(q, k, v, batch, seq):
    tq = SB_Q_ROWS
    n_q = seq // tq
    n_pairs = SB_HEADS // 2
    return pl.pallas_call(
        _sb_kernel,
        out_shape=jax.ShapeDtypeStruct(q.shape, BF16),
        grid=(batch, n_q),
        in_specs=[
            pl.BlockSpec((tq, SB_WIDTH), lambda b, i: (b * n_q + i, 0)),
            pl.BlockSpec((seq, SB_WIDTH), lambda b, i: (b, 0)),
            pl.BlockSpec((seq, SB_WIDTH), lambda b, i: (b, 0)),
        ],
        out_specs=pl.BlockSpec((tq, SB_WIDTH), lambda b, i: (b * n_q + i, 0)),
        scratch_shapes=[
            pltpu.VMEM((n_pairs, 2 * tq, LANES), BF16),
            pltpu.VMEM((tq, SB_WIDTH), F32),
            pltpu.VMEM((n_pairs, 2 * tq, LANES), F32),
            pltpu.VMEM((n_pairs, 2 * tq, tq + SB_BACK_KEYS), F32),
            pltpu.VMEM((n_pairs, 2 * tq, tq + SB_BACK_KEYS), BF16),
        ],
        compiler_params=pltpu.CompilerParams(
            dimension_semantics=("parallel", "parallel"), vmem_limit_bytes=VMEM_LIMIT_BYTES),
        name="stick_breaking_attention",
    )(q, k, v)


def _gla_kernel(q_ref, k_ref, v_ref, gg_ref, la_ref, ng_ref, o_ref, st_ref):
    @pl.when(pl.program_id(0) == 0)
    def _():
        st_ref[...] = jnp.zeros_like(st_ref)

    ch = GLA_CHUNK
    n_batch, n_rows = q_ref.shape[0], q_ref.shape[1]
    n_ch = n_rows // ch
    n_pairs = GLA_HEADS // 2
    shift = ch.bit_length() - 1
    r = lax.broadcasted_iota(jnp.int32, (n_rows, n_rows), 0)
    c = lax.broadcasted_iota(jnp.int32, (n_rows, n_rows), 1)
    same_chunk = jnp.right_shift(r, shift) == jnp.right_shift(c, shift)
    gate_mats = jnp.concatenate([
        jnp.where(jnp.logical_and(same_chunk, c <= r), 1.0, 0.0),
        jnp.where(jnp.logical_and(same_chunk, jnp.bitwise_and(c, ch - 1) < ch // 2), 1.0, 0.0),
        jnp.where(same_chunk, 1.0, 0.0)], axis=0).astype(BF16)
    r2 = lax.broadcasted_iota(jnp.int32, (2 * ch, 2 * ch), 0)
    c2 = lax.broadcasted_iota(jnp.int32, (2 * ch, 2 * ch), 1)
    same_head_causal = jnp.logical_and(jnp.bitwise_and(r2, ch) == jnp.bitwise_and(c2, ch),
                                       jnp.bitwise_and(c2, ch - 1) <= jnp.bitwise_and(r2, ch - 1))
    lane = lax.broadcasted_iota(jnp.int32, (1, LANES), 1)
    even_lanes = lane < GLA_KEY_DIM
    norm_g = ng_ref[...]

    def stack_masked(t):
        return jnp.concatenate([jnp.where(even_lanes, t, 0.0), jnp.where(even_lanes, 0.0, t)],
                               axis=0).astype(BF16)

    def stack_heads(t):
        return jnp.concatenate([t[:, :GLA_VAL_DIM], t[:, GLA_VAL_DIM:]], axis=0)

    bodies = [(bi, ci, p) for bi in range(n_batch) for ci in range(n_ch) for p in range(n_pairs)]
    rows_of = lambda ci: slice(ci * ch, (ci + 1) * ch)
    pair_of = lambda p: slice(p * LANES, (p + 1) * LANES)
    vcols_of = lambda p: slice(2 * p * GLA_VAL_DIM, (2 * p + 2) * GLA_VAL_DIM)

    gated = []
    for bi in range(n_batch):
        g_hi, g_lo = _split_bf16(la_ref[bi])
        b3 = _dot(gate_mats, g_hi) + _dot(gate_mats, g_lo)
        b, b_mid, b_last = b3[0:n_rows], b3[n_rows:2 * n_rows], b3[2 * n_rows:]
        q = q_ref[bi].astype(F32)
        k = k_ref[bi].astype(F32)
        gated.append(dict(q_in=q * jnp.exp(b - b_mid), k_in=(k * jnp.exp(b_mid - b)).astype(BF16),
                          k_dec=k * jnp.exp(b_last - b), q_b=q * jnp.exp(b), b_last=b_last))

    v_st, scores = {}, {}
    for bi, ci, p in bodies:
        rows, pair = rows_of(ci), pair_of(p)
        v_st[bi, ci, p] = stack_heads(v_ref[bi, rows, vcols_of(p)])
        k_pair = gated[bi]["k_in"][rows, pair]
        s = _dot_nt(stack_masked(gated[bi]["q_in"][rows, pair]), jnp.concatenate([k_pair, k_pair], axis=0))
        scores[bi, ci, p] = jnp.where(same_head_causal, s, 0.0).astype(BF16)

    o_intra, upd = {}, {}
    for body in bodies:
        bi, ci, p = body
        o_intra[body] = _dot(scores[body], v_st[body])
        upd[body] = _dot(v_st[body].astype(F32).T.astype(BF16),
                         stack_masked(gated[bi]["k_dec"][rows_of(ci), pair_of(p)]))

    st_in = {}
    for bi in range(n_batch):
        for p in range(n_pairs):
            st = st_ref[bi, p]
            for ci in range(n_ch):
                st_in[bi, ci, p] = st
                decay = jnp.exp(gated[bi]["b_last"][ci * ch:ci * ch + 1, pair_of(p)])
                st = st * decay + upd[bi, ci, p]
            st_ref[bi, p] = st

    for body in bodies:
        bi, ci, p = body
        rows = rows_of(ci)
        o = o_intra[body] + _dot_nt(stack_masked(gated[bi]["q_b"][rows, pair_of(p)]),
                                    st_in[body].astype(BF16))
        on = o * lax.rsqrt(jnp.mean(o * o, axis=-1, keepdims=True) + RMS_EPS) * norm_g
        gate = stack_heads(gg_ref[bi, rows, vcols_of(p)].astype(F32))
        res = (on * (gate / (1.0 + jnp.exp(-gate)))).astype(o_ref.dtype)
        o_ref[bi, rows, vcols_of(p)] = jnp.concatenate([res[:ch], res[ch:]], axis=1)


def _gla(gq, gk, gv, gg, log_a, norm_g, batch, seq):
    rows = GLA_ROWS
    shaped = lambda t: t.reshape(batch, seq, t.shape[-1])
    tile = lambda i: (0, i, 0)
    out = pl.pallas_call(
        _gla_kernel,
        out_shape=jax.ShapeDtypeStruct((batch, seq, GLA_WIDTH), BF16),
        grid=(seq // rows,),
        in_specs=[
            pl.BlockSpec((batch, rows, GLA_KEYS), tile),
            pl.BlockSpec((batch, rows, GLA_KEYS), tile),
            pl.BlockSpec((batch, rows, GLA_WIDTH), tile),
            pl.BlockSpec((batch, rows, GLA_WIDTH), tile),
            pl.BlockSpec((batch, rows, GLA_KEYS), tile),
            pl.BlockSpec((1, GLA_VAL_DIM), lambda i: (0, 0)),
        ],
        out_specs=pl.BlockSpec((batch, rows, GLA_WIDTH), tile),
        scratch_shapes=[pltpu.VMEM((batch, GLA_HEADS // 2, GLA_VAL_DIM, LANES), F32)],
        compiler_params=pltpu.CompilerParams(
            dimension_semantics=("arbitrary",), vmem_limit_bytes=VMEM_LIMIT_BYTES),
        name="gated_linear_attention",
    )(shaped(gq), shaped(gk), shaped(gv), shaped(gg), shaped(log_a), norm_g)
    return out.reshape(batch * seq, GLA_WIDTH)


def _mix_ffn_kernel(alpha, tiles_per_seq, sb_ref, gla_ref, x_ref, wo_ref, g1_ref, b1_ref,
                    wup_ref, cw_ref, cb_ref, wd_ref, g2_ref, b2_ref, o_ref,
                    h_ref, ua_ref, uc_ref, act_ref):
    rows = x_ref.shape[0]
    n_chunks = D_FF // FFN_CHUNK

    @pl.when(pl.program_id(0) % tiles_per_seq == 0)
    def _():
        ua_ref[:, 0:CONV_TAIL, :] = jnp.zeros((n_chunks, CONV_TAIL, FFN_CHUNK), F32)
        uc_ref[:, 0:CONV_TAIL, :] = jnp.zeros((n_chunks, CONV_TAIL, FFN_CHUNK), F32)

    mix = _dot(sb_ref[...], wo_ref[0:SB_WIDTH, :]) + _dot(gla_ref[...], wo_ref[SB_WIDTH:, :])
    h = _layer_norm(alpha * x_ref[...] + mix, g1_ref[...], b1_ref[...])
    h_ref[...] = h
    hb = h.astype(BF16)

    def conv(u_ref, slot, c0):
        w = cw_ref[:, c0:c0 + FFN_CHUNK]
        return (w[2:3, :] * u_ref[slot, CONV_TAIL:CONV_TAIL + rows, :]
                + w[1:2, :] * u_ref[slot, CONV_TAIL - 1:CONV_TAIL - 1 + rows, :]
                + w[0:1, :] * u_ref[slot, CONV_TAIL - 2:CONV_TAIL - 2 + rows, :]
                + cb_ref[:, c0:c0 + FFN_CHUNK])

    for ci in range(n_chunks):
        a0 = ci * FFN_CHUNK
        c0 = D_FF + ci * FFN_CHUNK
        ua_ref[ci, CONV_TAIL:, :] = _dot(hb, wup_ref[:, a0:a0 + FFN_CHUNK])
        uc_ref[ci, CONV_TAIL:, :] = _dot(hb, wup_ref[:, c0:c0 + FFN_CHUNK])
        a = conv(ua_ref, ci, a0)
        c = conv(uc_ref, ci, c0)
        act_ref[:, a0:a0 + FFN_CHUNK] = (0.5 * a * (1.0 + lax.erf(a * (2.0 ** -0.5))) * c).astype(BF16)
        for u_ref in (ua_ref, uc_ref):
            u_ref[ci, 0:CONV_TAIL, :] = u_ref[ci, rows:rows + CONV_TAIL, :]

    f = _dot(act_ref[...], wd_ref[...])
    o_ref[...] = _layer_norm(alpha * h_ref[...] + f, g2_ref[...], b2_ref[...])


def _mix_ffn(sb_o, gla_o, x, w_out, g1, b1, w_up, conv_w, conv_b, w_down, g2, b2, alpha, seq):
    tokens = x.shape[0]
    rows = FFN_ROWS
    n_chunks = D_FF // FFN_CHUNK
    assert n_chunks * FFN_CHUNK == D_FF and seq % rows == 0
    tile = lambda i: (i, 0)
    const = lambda i: (0, 0)
    whole = lambda t: pl.BlockSpec(t.shape, const)
    return pl.pallas_call(
        functools.partial(_mix_ffn_kernel, alpha, seq // rows),
        out_shape=jax.ShapeDtypeStruct((tokens, D_MODEL), F32),
        grid=(tokens // rows,),
        in_specs=[
            pl.BlockSpec((rows, SB_WIDTH), tile),
            pl.BlockSpec((rows, GLA_WIDTH), tile),
            pl.BlockSpec((rows, D_MODEL), tile),
            whole(w_out), whole(g1), whole(b1),
            whole(w_up), whole(conv_w), whole(conv_b), whole(w_down), whole(g2), whole(b2),
        ],
        out_specs=pl.BlockSpec((rows, D_MODEL), tile),
        scratch_shapes=[
            pltpu.VMEM((rows, D_MODEL), F32),
            pltpu.VMEM((n_chunks, CONV_TAIL + rows, FFN_CHUNK), F32),
            pltpu.VMEM((n_chunks, CONV_TAIL + rows, FFN_CHUNK), F32),
            pltpu.VMEM((rows, D_FF), BF16),
        ],
        compiler_params=pltpu.CompilerParams(
            dimension_semantics=("arbitrary",), vmem_limit_bytes=VMEM_LIMIT_BYTES),
        name="mix_ffn",
    )(sb_o, gla_o, x, w_out, g1, b1, w_up, conv_w, conv_b, w_down, g2, b2)


def _layer(x, w_in, gate_up, gate_bias, gla_norm_g, w_out, ln1_g, ln1_b,
           w_up, conv_w, conv_b, w_down, ln2_g, ln2_b, alpha, batch, seq):
    row = lambda t: t.reshape(1, -1)
    q, k, v, gq, gk, gv, gg, log_a = _input_projection(
        x, w_in.astype(BF16), gate_up, row(gate_bias))
    sb_o = _stick_breaking(q, k, v, batch, seq)
    gla_o = _gla(gq, gk, gv, gg, log_a, row(gla_norm_g), batch, seq)
    return _mix_ffn(sb_o, gla_o, x, w_out.astype(BF16), row(ln1_g), row(ln1_b),
                    w_up.astype(BF16), conv_w, row(conv_b), w_down.astype(BF16),
                    row(ln2_g), row(ln2_b), alpha, seq)


def kernel(x, w_in, gate_up, gate_bias, gla_norm_g, w_out, ln1_g, ln1_b, w_up, conv_w, conv_b,
           w_down, ln2_g, ln2_b):
    batch, seq, d_model = x.shape
    depth = w_in.shape[0]
    alpha = (2.0 * depth) ** 0.25
    y = x.reshape(batch * seq, d_model)
    for l in range(depth):
        y = _layer(y, w_in[l], gate_up[l], gate_bias[l], gla_norm_g[l], w_out[l], ln1_g[l], ln1_b[l],
                   w_up[l], conv_w[l], conv_b[l], w_down[l], ln2_g[l], ln2_b[l], alpha, batch, seq)
    return y.reshape(batch, seq, d_model)
```

```python
import functools

import jax
import jax.numpy as jnp
from jax import lax
from jax.experimental import pallas as pl
from jax.experimental.pallas import tpu as pltpu

F32 = jnp.float32
BF16 = jnp.bfloat16

D_MODEL = 1024
SB_WIDTH = 512
SB_HEADS = 8
SB_HEAD_DIM = 64
GLA_WIDTH = 512
GLA_HEADS = 4
GLA_KEY_DIM = 64
GLA_VAL_DIM = 128
GLA_KEYS = GLA_HEADS * GLA_KEY_DIM
GLA_GATE_RANK = 16
GLA_TAU = 16.0
GLA_CHUNK = 64
D_FF = 2816
CONV_WIDTH = 3
LN_EPS = 1e-5
RMS_EPS = 1e-6
MAIN_WIDTH = 3 * SB_WIDTH + 2 * GLA_KEYS + 2 * GLA_WIDTH

LANES = 128
SUBLANES = 8
MXU_WIDTH = 256
VMEM_LIMIT_BYTES = 56 * 1024 * 1024

PROJ_ROWS = 1024
SB_Q_ROWS = 128
SB_BACK_KEYS = 256
SB_TILES_PER_STEP = 2
GLA_ROWS = 256
FFN_ROWS = 512
FFN_CHUNK = MXU_WIDTH
CONV_TAIL = SUBLANES

SB_DEAD_LOG_WEIGHT = -90.0
SB_MASKED_LOG_WEIGHT = -1e30
F32_SIGN_BIT = -2 ** 31

NT_DIMS = (((1,), (1,)), ((), ()))


def _split_bf16(x):
    hi = x.astype(BF16)
    lo = (x - hi.astype(F32)).astype(BF16)
    return hi, lo


def _dot(a, b):
    return jnp.dot(a, b, preferred_element_type=F32)


def _dot_nt(a, b):
    return lax.dot_general(a, b, NT_DIMS, preferred_element_type=F32)


def _softplus(z):
    neg_abs = lax.bitcast_convert_type(
        lax.bitcast_convert_type(z, jnp.int32) | jnp.int32(F32_SIGN_BIT), F32)
    return jnp.maximum(z, 0.0) + jnp.log(1.0 + jnp.exp(neg_abs))


def _layer_norm(y, g, b):
    mu = jnp.mean(y, axis=-1, keepdims=True)
    d = y - mu
    var = jnp.mean(d * d, axis=-1, keepdims=True)
    return d * lax.rsqrt(var + LN_EPS) * g + b


def _proj_kernel(x_ref, w32_ref, gup_ref, gbias_ref,
                 q_ref, k_ref, v_ref, gq_ref, gk_ref, gv_ref, gg_ref, la_ref, w_ref):
    @pl.when(pl.program_id(0) == 0)
    def _():
        w_ref[...] = w32_ref[...].astype(BF16)

    xb = x_ref[...].astype(BF16)
    col = 0
    for out_ref, scale in ((q_ref, SB_HEAD_DIM ** -0.5), (k_ref, None), (v_ref, None),
                           (gq_ref, GLA_KEY_DIM ** -0.5), (gk_ref, None), (gv_ref, None),
                           (gg_ref, None)):
        width = out_ref.shape[1]
        y = _dot(xb, w_ref[:, col:col + width])
        if scale is not None:
            y = y * scale
        out_ref[...] = y.astype(out_ref.dtype)
        col += width
    ga = _dot(xb, w_ref[:, col:col + GLA_GATE_RANK])
    logits = _dot(ga.astype(BF16), gup_ref[...].astype(BF16)) + gbias_ref[...]
    la_ref[...] = -_softplus(-logits) * (1.0 / GLA_TAU)


def _input_projection(x, w_in, gate_up, gate_bias):
    tokens = x.shape[0]
    rows = PROJ_ROWS
    widths = (SB_WIDTH, SB_WIDTH, SB_WIDTH, GLA_KEYS, GLA_KEYS, GLA_WIDTH, GLA_WIDTH)
    const = lambda i: (0, 0)
    tile = lambda i: (i, 0)
    out_shape = [jax.ShapeDtypeStruct((tokens, w), BF16) for w in widths]
    out_shape.append(jax.ShapeDtypeStruct((tokens, GLA_KEYS), F32))
    out_specs = [pl.BlockSpec((rows, w), tile) for w in widths]
    out_specs.append(pl.BlockSpec((rows, GLA_KEYS), tile))
    return pl.pallas_call(
        _proj_kernel,
        out_shape=out_shape,
        grid=(tokens // rows,),
        in_specs=[
            pl.BlockSpec((rows, D_MODEL), tile),
            pl.BlockSpec((D_MODEL, MAIN_WIDTH + GLA_GATE_RANK), const),
            pl.BlockSpec((GLA_GATE_RANK, GLA_KEYS), const),
            pl.BlockSpec((1, GLA_KEYS), const),
        ],
        out_specs=out_specs,
        scratch_shapes=[pltpu.VMEM((D_MODEL, MAIN_WIDTH + GLA_GATE_RANK), BF16)],
        compiler_params=pltpu.CompilerParams(
            dimension_semantics=("arbitrary",), vmem_limit_bytes=VMEM_LIMIT_BYTES),
        name="input_projection",
    )(x, w_in, gate_up, gate_bias)


def _sb_kernel(q_ref, k_ref, v_ref, o_ref, qm_ref, acc_ref, carry_ref, z_ref, sp_ref):
    qi = pl.program_id(1)
    tq = SB_Q_ROWS
    n_sub = q_ref.shape[0] // tq
    wide = SB_BACK_KEYS
    n_pairs = SB_HEADS // 2
    lane = lax.broadcasted_iota(jnp.int32, (1, LANES), 1)
    even_lanes = lane < SB_HEAD_DIM

    def neg_suffix_mat(width):
        row = lax.broadcasted_iota(jnp.int32, (width, width), 0)
        col = lax.broadcasted_iota(jnp.int32, (width, width), 1)
        return jnp.where(row >= col, -1.0, 0.0).astype(BF16)

    suffix_mats = {tq: neg_suffix_mat(tq), wide: neg_suffix_mat(wide)}

    for s in range(n_sub):
        for p in range(n_pairs):
            qp = q_ref[s * tq:(s + 1) * tq, p * LANES:(p + 1) * LANES].astype(F32)
            qm_ref[s, p, 0:tq, :] = jnp.where(even_lanes, qp, 0.0).astype(BF16)
            qm_ref[s, p, tq:2 * tq, :] = jnp.where(even_lanes, 0.0, qp).astype(BF16)

    def visit_blocks(work, fresh):
        col0 = [0] * n_sub
        placed = []
        for s, key_start, width, diagonal in work:
            cols = slice(col0[s], col0[s] + width)
            strict = None
            if diagonal:
                r2 = lax.broadcasted_iota(jnp.int32, (2 * tq, width), 0)
                c2 = lax.broadcasted_iota(jnp.int32, (2 * tq, width), 1)
                strict = c2 < jnp.bitwise_and(r2, tq - 1)
            for p in range(n_pairs):
                kp = k_ref[pl.ds(key_start, width), p * LANES:(p + 1) * LANES]
                z = _dot_nt(qm_ref[s, p], kp)
                sp = _softplus(z)
                z_ref[s, p, :, cols] = z
                if diagonal:
                    sp = jnp.where(strict, sp, 0.0)
                sp_ref[s, p, :, cols] = sp.astype(BF16)
            placed.append((s, key_start, width, strict, cols, fresh and col0[s] == 0))
            col0[s] += width
        for s, key_start, width, strict, cols, first in placed:
            for p in range(n_pairs):
                pcols = slice(p * LANES, (p + 1) * LANES)
                incl = _dot(sp_ref[s, p, :, cols], suffix_mats[width])
                log_w = z_ref[s, p, :, cols] + incl
                if not first:
                    carry = carry_ref[s, p]
                    log_w = log_w + jnp.concatenate([carry] * (width // LANES), axis=1)
                if strict is not None:
                    log_w = jnp.where(strict, log_w, SB_MASKED_LOG_WEIGHT)
                w = jnp.exp(log_w).astype(BF16)
                d = _dot(w, v_ref[pl.ds(key_start, width), pcols])
                pv = jnp.where(even_lanes, d[0:tq], d[tq:2 * tq])
                block_total = incl[:, 0:1]
                if first:
                    acc_ref[s, :, pcols] = pv
                    carry_ref[s, p] = jnp.broadcast_to(block_total, (2 * tq, LANES))
                else:
                    acc_ref[s, :, pcols] += pv
                    carry_ref[s, p] = carry + block_total

    def any_row_alive(s):
        m = carry_ref[s, 0]
        for p in range(1, n_pairs):
            m = jnp.maximum(m, carry_ref[s, p])
        return (jnp.max(m) > SB_DEAD_LOG_WEIGHT).astype(jnp.int32)

    tile_index = [qi * n_sub + s for s in range(n_sub)]
    q0 = [pl.multiple_of(t * tq, tq) for t in tile_index]
    has_wide = qi * n_sub >= wide // tq

    @pl.when(has_wide)
    def _():
        work = []
        for s in range(n_sub):
            work += [(s, q0[s], tq, True), (s, pl.multiple_of(q0[s] - wide, tq), wide, False)]
        visit_blocks(work, fresh=True)

    @pl.when(jnp.logical_not(has_wide))
    def _():
        visit_blocks([(s, q0[s], tq, True) for s in range(n_sub)], fresh=True)

    def cond(state):
        kb, alive = state
        return jnp.logical_and(kb >= 0, alive > 0)

    alive = [any_row_alive(s) for s in range(n_sub)]
    for s in range(n_sub):
        def body(state, s=s):
            kb, _ = state
            visit_blocks([(s, pl.multiple_of(kb * tq, tq), tq, False)], fresh=False)
            return kb - 1, any_row_alive(s)

        next_block = jnp.where(has_wide, tile_index[s] - wide // tq, tile_index[s]) - 1
        lax.while_loop(cond, body, (next_block, alive[s]))
        o_ref[s * tq:(s + 1) * tq, :] = acc_ref[s].astype(o_ref.dtype)


def _stick_breaking(q, k, v, batch, seq):
    tq = SB_Q_ROWS
    n_sub = SB_TILES_PER_STEP
    assert (SB_BACK_KEYS // tq) % n_sub == 0
    n_q = seq // (tq * n_sub)
    n_pairs = SB_HEADS // 2
    return pl.pallas_call(
        _sb_kernel,
        out_shape=jax.ShapeDtypeStruct(q.shape, BF16),
        grid=(batch, n_q),
        in_specs=[
            pl.BlockSpec((n_sub * tq, SB_WIDTH), lambda b, i: (b * n_q + i, 0)),
            pl.BlockSpec((seq, SB_WIDTH), lambda b, i: (b, 0)),
            pl.BlockSpec((seq, SB_WIDTH), lambda b, i: (b, 0)),
        ],
        out_specs=pl.BlockSpec((n_sub * tq, SB_WIDTH), lambda b, i: (b * n_q + i, 0)),
        scratch_shapes=[
            pltpu.VMEM((n_sub, n_pairs, 2 * tq, LANES), BF16),
            pltpu.VMEM((n_sub, tq, SB_WIDTH), F32),
            pltpu.VMEM((n_sub, n_pairs, 2 * tq, LANES), F32),
            pltpu.VMEM((n_sub, n_pairs, 2 * tq, tq + SB_BACK_KEYS), F32),
            pltpu.VMEM((n_sub, n_pairs, 2 * tq, tq + SB_BACK_KEYS), BF16),
        ],
        compiler_params=pltpu.CompilerParams(
            dimension_semantics=("parallel", "parallel"), vmem_limit_bytes=VMEM_LIMIT_BYTES),
        name="stick_breaking_attention",
    )(q, k, v)


def _gla_kernel(q_ref, k_ref, v_ref, gg_ref, la_ref, ng_ref, o_ref, st_ref):
    @pl.when(pl.program_id(0) == 0)
    def _():
        st_ref[...] = jnp.zeros_like(st_ref)

    ch = GLA_CHUNK
    n_batch, n_rows = q_ref.shape[0], q_ref.shape[1]
    n_ch = n_rows // ch
    n_pairs = GLA_HEADS // 2
    shift = ch.bit_length() - 1
    r = lax.broadcasted_iota(jnp.int32, (n_rows, n_rows), 0)
    c = lax.broadcasted_iota(jnp.int32, (n_rows, n_rows), 1)
    same_chunk = jnp.right_shift(r, shift) == jnp.right_shift(c, shift)
    gate_mats = jnp.concatenate([
        jnp.where(jnp.logical_and(same_chunk, c <= r), 1.0, 0.0),
        jnp.where(jnp.logical_and(same_chunk, jnp.bitwise_and(c, ch - 1) < ch // 2), 1.0, 0.0),
        jnp.where(same_chunk, 1.0, 0.0)], axis=0).astype(BF16)
    r2 = lax.broadcasted_iota(jnp.int32, (2 * ch, 2 * ch), 0)
    c2 = lax.broadcasted_iota(jnp.int32, (2 * ch, 2 * ch), 1)
    same_head_causal = jnp.logical_and(jnp.bitwise_and(r2, ch) == jnp.bitwise_and(c2, ch),
                                       jnp.bitwise_and(c2, ch - 1) <= jnp.bitwise_and(r2, ch - 1))
    lane = lax.broadcasted_iota(jnp.int32, (1, LANES), 1)
    even_lanes = lane < GLA_KEY_DIM
    norm_g = ng_ref[...]

    def stack_masked(t):
        return jnp.concatenate([jnp.where(even_lanes, t, 0.0), jnp.where(even_lanes, 0.0, t)],
                               axis=0).astype(BF16)

    def stack_heads(t):
        return jnp.concatenate([t[:, :GLA_VAL_DIM], t[:, GLA_VAL_DIM:]], axis=0)

    bodies = [(bi, ci, p) for bi in range(n_batch) for ci in range(n_ch) for p in range(n_pairs)]
    rows_of = lambda ci: slice(ci * ch, (ci + 1) * ch)
    pair_of = lambda p: slice(p * LANES, (p + 1) * LANES)
    vcols_of = lambda p: slice(2 * p * GLA_VAL_DIM, (2 * p + 2) * GLA_VAL_DIM)

    gated = []
    for bi in range(n_batch):
        g_hi, g_lo = _split_bf16(la_ref[bi])
        b3 = _dot(gate_mats, g_hi) + _dot(gate_mats, g_lo)
        b, b_mid, b_last = b3[0:n_rows], b3[n_rows:2 * n_rows], b3[2 * n_rows:]
        q = q_ref[bi].astype(F32)
        k = k_ref[bi].astype(F32)
        gated.append(dict(q_in=q * jnp.exp(b - b_mid), k_in=(k * jnp.exp(b_mid - b)).astype(BF16),
                          k_dec=k * jnp.exp(b_last - b), q_b=q * jnp.exp(b), b_last=b_last))

    v_st, scores = {}, {}
    for bi, ci, p in bodies:
        rows, pair = rows_of(ci), pair_of(p)
        v_st[bi, ci, p] = stack_heads(v_ref[bi, rows, vcols_of(p)])
        k_pair = gated[bi]["k_in"][rows, pair]
        s = _dot_nt(stack_masked(gated[bi]["q_in"][rows, pair]), jnp.concatenate([k_pair, k_pair], axis=0))
        scores[bi, ci, p] = jnp.where(same_head_causal, s, 0.0).astype(BF16)

    o_intra, upd = {}, {}
    for body in bodies:
        bi, ci, p = body
        o_intra[body] = _dot(scores[body], v_st[body])
        upd[body] = _dot(v_st[body].astype(F32).T.astype(BF16),
                         stack_masked(gated[bi]["k_dec"][rows_of(ci), pair_of(p)]))

    st_in = {}
    for bi in range(n_batch):
        for p in range(n_pairs):
            st = st_ref[bi, p]
            for ci in range(n_ch):
                st_in[bi, ci, p] = st
                decay = jnp.exp(gated[bi]["b_last"][ci * ch:ci * ch + 1, pair_of(p)])
                st = st * decay + upd[bi, ci, p]
            st_ref[bi, p] = st

    for body in bodies:
        bi, ci, p = body
        rows = rows_of(ci)
        o = o_intra[body] + _dot_nt(stack_masked(gated[bi]["q_b"][rows, pair_of(p)]),
                                    st_in[body].astype(BF16))
        on = o * lax.rsqrt(jnp.mean(o * o, axis=-1, keepdims=True) + RMS_EPS) * norm_g
        gate = stack_heads(gg_ref[bi, rows, vcols_of(p)].astype(F32))
        res = (on * (gate / (1.0 + jnp.exp(-gate)))).astype(o_ref.dtype)
        o_ref[bi, rows, vcols_of(p)] = jnp.concatenate([res[:ch], res[ch:]], axis=1)


def _gla(gq, gk, gv, gg, log_a, norm_g, batch, seq):
    rows = GLA_ROWS
    shaped = lambda t: t.reshape(batch, seq, t.shape[-1])
    tile = lambda i: (0, i, 0)
    out = pl.pallas_call(
        _gla_kernel,
        out_shape=jax.ShapeDtypeStruct((batch, seq, GLA_WIDTH), BF16),
        grid=(seq // rows,),
        in_specs=[
            pl.BlockSpec((batch, rows, GLA_KEYS), tile),
            pl.BlockSpec((batch, rows, GLA_KEYS), tile),
            pl.BlockSpec((batch, rows, GLA_WIDTH), tile),
            pl.BlockSpec((batch, rows, GLA_WIDTH), tile),
            pl.BlockSpec((batch, rows, GLA_KEYS), tile),
            pl.BlockSpec((1, GLA_VAL_DIM), lambda i: (0, 0)),
        ],
        out_specs=pl.BlockSpec((batch, rows, GLA_WIDTH), tile),
        scratch_shapes=[pltpu.VMEM((batch, GLA_HEADS // 2, GLA_VAL_DIM, LANES), F32)],
        compiler_params=pltpu.CompilerParams(
            dimension_semantics=("arbitrary",), vmem_limit_bytes=VMEM_LIMIT_BYTES),
        name="gated_linear_attention",
    )(shaped(gq), shaped(gk), shaped(gv), shaped(gg), shaped(log_a), norm_g)
    return out.reshape(batch * seq, GLA_WIDTH)


def _mix_ffn_kernel(alpha, tiles_per_seq, sb_ref, gla_ref, x_ref, wo_ref, g1_ref, b1_ref,
                    wup_ref, cw_ref, cb_ref, wd_ref, g2_ref, b2_ref, o_ref,
                    h_ref, ua_ref, uc_ref, act_ref):
    rows = x_ref.shape[0]
    n_chunks = D_FF // FFN_CHUNK

    @pl.when(pl.program_id(0) % tiles_per_seq == 0)
    def _():
        ua_ref[:, 0:CONV_TAIL, :] = jnp.zeros((n_chunks, CONV_TAIL, FFN_CHUNK), F32)
        uc_ref[:, 0:CONV_TAIL, :] = jnp.zeros((n_chunks, CONV_TAIL, FFN_CHUNK), F32)

    mix = _dot(sb_ref[...], wo_ref[0:SB_WIDTH, :]) + _dot(gla_ref[...], wo_ref[SB_WIDTH:, :])
    h = _layer_norm(alpha * x_ref[...] + mix, g1_ref[...], b1_ref[...])
    h_ref[...] = h
    hb = h.astype(BF16)

    def conv(u_ref, slot, c0, scale):
        w = cw_ref[:, c0:c0 + FFN_CHUNK] * scale
        return (w[2:3, :] * u_ref[slot, CONV_TAIL:CONV_TAIL + rows, :]
                + w[1:2, :] * u_ref[slot, CONV_TAIL - 1:CONV_TAIL - 1 + rows, :]
                + w[0:1, :] * u_ref[slot, CONV_TAIL - 2:CONV_TAIL - 2 + rows, :]
                + cb_ref[:, c0:c0 + FFN_CHUNK] * scale)

    for ci in range(n_chunks):
        a0 = ci * FFN_CHUNK
        c0 = D_FF + ci * FFN_CHUNK
        ua_ref[ci, CONV_TAIL:, :] = _dot(hb, wup_ref[:, a0:a0 + FFN_CHUNK])
        uc_ref[ci, CONV_TAIL:, :] = _dot(hb, wup_ref[:, c0:c0 + FFN_CHUNK])
        half_a = conv(ua_ref, ci, a0, 0.5)
        c = conv(uc_ref, ci, c0, 1.0)
        act_ref[:, a0:a0 + FFN_CHUNK] = (
            half_a * (1.0 + lax.erf(half_a * (2.0 ** 0.5))) * c).astype(BF16)
        for u_ref in (ua_ref, uc_ref):
            u_ref[ci, 0:CONV_TAIL, :] = u_ref[ci, rows:rows + CONV_TAIL, :]

    f = _dot(act_ref[...], wd_ref[...])
    o_ref[...] = _layer_norm(alpha * h_ref[...] + f, g2_ref[...], b2_ref[...])


def _mix_ffn(sb_o, gla_o, x, w_out, g1, b1, w_up, conv_w, conv_b, w_down, g2, b2, alpha, seq):
    tokens = x.shape[0]
    rows = FFN_ROWS
    n_chunks = D_FF // FFN_CHUNK
    assert n_chunks * FFN_CHUNK == D_FF and seq % rows == 0
    tile = lambda i: (i, 0)
    const = lambda i: (0, 0)
    whole = lambda t: pl.BlockSpec(t.shape, const)
    return pl.pallas_call(
        functools.partial(_mix_ffn_kernel, alpha, seq // rows),
        out_shape=jax.ShapeDtypeStruct((tokens, D_MODEL), F32),
        grid=(tokens // rows,),
        in_specs=[
            pl.BlockSpec((rows, SB_WIDTH), tile),
            pl.BlockSpec((rows, GLA_WIDTH), tile),
            pl.BlockSpec((rows, D_MODEL), tile),
            whole(w_out), whole(g1), whole(b1),
            whole(w_up), whole(conv_w), whole(conv_b), whole(w_down), whole(g2), whole(b2),
        ],
        out_specs=pl.BlockSpec((rows, D_MODEL), tile),
        scratch_shapes=[
            pltpu.VMEM((rows, D_MODEL), F32),
            pltpu.VMEM((n_chunks, CONV_TAIL + rows, FFN_CHUNK), F32),
            pltpu.VMEM((n_chunks, CONV_TAIL + rows, FFN_CHUNK), F32),
            pltpu.VMEM((rows, D_FF), BF16),
        ],
        compiler_params=pltpu.CompilerParams(
            dimension_semantics=("arbitrary",), vmem_limit_bytes=VMEM_LIMIT_BYTES),
        name="mix_ffn",
    )(sb_o, gla_o, x, w_out, g1, b1, w_up, conv_w, conv_b, w_down, g2, b2)


def _layer(x, w_in, gate_up, gate_bias, gla_norm_g, w_out, ln1_g, ln1_b,
           w_up, conv_w, conv_b, w_down, ln2_g, ln2_b, alpha, batch, seq):
    row = lambda t: t.reshape(1, -1)
    q, k, v, gq, gk, gv, gg, log_a = _input_projection(
        x, w_in, gate_up, row(gate_bias))
    sb_o = _stick_breaking(q, k, v, batch, seq)
    gla_o = _gla(gq, gk, gv, gg, log_a, row(gla_norm_g), batch, seq)
    return _mix_ffn(sb_o, gla_o, x, w_out.astype(BF16), row(ln1_g), row(ln1_b),
                    w_up.astype(BF16), conv_w, row(conv_b), w_down.astype(BF16),
                    row(ln2_g), row(ln2_b), alpha, seq)


def kernel(x, w_in, gate_up, gate_bias, gla_norm_g, w_out, ln1_g, ln1_b, w_up, conv_w, conv_b,
           w_down, ln2_g, ln2_b):
    batch, seq, d_model = x.shape
    depth = w_in.shape[0]
    alpha = (2.0 * depth) ** 0.25
    y = x.reshape(batch * seq, d_model)
    for l in range(depth):
        y = _layer(y, w_in[l], gate_up[l], gate_bias[l], gla_norm_g[l], w_out[l], ln1_g[l], ln1_b[l],
                   w_up[l], conv_w[l], conv_b[l], w_down[l], ln2_g[l], ln2_b[l], alpha, batch, seq)
    return y.reshape(batch, seq, d_model)
```

```python
import functools

import jax
import jax.numpy as jnp
from jax import lax
from jax.experimental import pallas as pl
from jax.experimental.pallas import tpu as pltpu

F32 = jnp.float32
BF16 = jnp.bfloat16

D_MODEL = 1024
SB_WIDTH = 512
SB_HEADS = 8
SB_HEAD_DIM = 64
GLA_WIDTH = 512
GLA_HEADS = 4
GLA_KEY_DIM = 64
GLA_VAL_DIM = 128
GLA_KEYS = GLA_HEADS * GLA_KEY_DIM
GLA_GATE_RANK = 16
GLA_TAU = 16.0
GLA_CHUNK = 64
D_FF = 2816
CONV_WIDTH = 3
LN_EPS = 1e-5
RMS_EPS = 1e-6
MAIN_WIDTH = 3 * SB_WIDTH + 2 * GLA_KEYS + 2 * GLA_WIDTH

LANES = 128
SUBLANES = 8
MXU_WIDTH = 256
VMEM_LIMIT_BYTES = 56 * 1024 * 1024

PROJ_ROWS = 1024
SB_Q_ROWS = 128
SB_BACK_KEYS = 256
SB_TILES_PER_STEP = 4
GLA_ROWS = 256
FFN_ROWS = 512
FFN_CHUNK = MXU_WIDTH
CONV_TAIL = SUBLANES

SB_DEAD_LOG_WEIGHT = -90.0
SB_MASKED_LOG_WEIGHT = -1e30
F32_SIGN_BIT = -2 ** 31

NT_DIMS = (((1,), (1,)), ((), ()))


def _split_bf16(x):
    hi = x.astype(BF16)
    lo = (x - hi.astype(F32)).astype(BF16)
    return hi, lo


def _dot(a, b):
    return jnp.dot(a, b, preferred_element_type=F32)


def _dot_nt(a, b):
    return lax.dot_general(a, b, NT_DIMS, preferred_element_type=F32)


def _softplus(z):
    neg_abs = lax.bitcast_convert_type(
        lax.bitcast_convert_type(z, jnp.int32) | jnp.int32(F32_SIGN_BIT), F32)
    return jnp.maximum(z, 0.0) + jnp.log(1.0 + jnp.exp(neg_abs))


def _layer_norm(y, g, b):
    mu = jnp.mean(y, axis=-1, keepdims=True)
    d = y - mu
    var = jnp.mean(d * d, axis=-1, keepdims=True)
    return d * lax.rsqrt(var + LN_EPS) * g + b


def _proj_kernel(x_ref, w32_ref, gup_ref, gbias_ref,
                 q_ref, k_ref, v_ref, gq_ref, gk_ref, gv_ref, gg_ref, la_ref, w_ref):
    @pl.when(pl.program_id(0) == 0)
    def _():
        w_ref[...] = w32_ref[...].astype(BF16)

    xb = x_ref[...].astype(BF16)
    col = 0
    for out_ref, scale in ((q_ref, SB_HEAD_DIM ** -0.5), (k_ref, None), (v_ref, None),
                           (gq_ref, GLA_KEY_DIM ** -0.5), (gk_ref, None), (gv_ref, None),
                           (gg_ref, None)):
        width = out_ref.shape[1]
        y = _dot(xb, w_ref[:, col:col + width])
        if scale is not None:
            y = y * scale
        out_ref[...] = y.astype(out_ref.dtype)
        col += width
    ga = _dot(xb, w_ref[:, col:col + GLA_GATE_RANK])
    logits = _dot(ga.astype(BF16), gup_ref[...].astype(BF16)) + gbias_ref[...]
    la_ref[...] = -_softplus(-logits) * (1.0 / GLA_TAU)


def _input_projection(x, w_in, layer, gate_up, gate_bias):
    tokens = x.shape[0]
    rows = PROJ_ROWS
    widths = (SB_WIDTH, SB_WIDTH, SB_WIDTH, GLA_KEYS, GLA_KEYS, GLA_WIDTH, GLA_WIDTH)
    const = lambda i: (0, 0)
    tile = lambda i: (i, 0)
    out_shape = [jax.ShapeDtypeStruct((tokens, w), BF16) for w in widths]
    out_shape.append(jax.ShapeDtypeStruct((tokens, GLA_KEYS), F32))
    out_specs = [pl.BlockSpec((rows, w), tile) for w in widths]
    out_specs.append(pl.BlockSpec((rows, GLA_KEYS), tile))
    return pl.pallas_call(
        _proj_kernel,
        out_shape=out_shape,
        grid=(tokens // rows,),
        in_specs=[
            pl.BlockSpec((rows, D_MODEL), tile),
            pl.BlockSpec((None, D_MODEL, MAIN_WIDTH + GLA_GATE_RANK), lambda i: (layer, 0, 0)),
            pl.BlockSpec((GLA_GATE_RANK, GLA_KEYS), const),
            pl.BlockSpec((1, GLA_KEYS), const),
        ],
        out_specs=out_specs,
        scratch_shapes=[pltpu.VMEM((D_MODEL, MAIN_WIDTH + GLA_GATE_RANK), BF16)],
        compiler_params=pltpu.CompilerParams(
            dimension_semantics=("arbitrary",), vmem_limit_bytes=VMEM_LIMIT_BYTES),
        name="input_projection",
    )(x, w_in, gate_up, gate_bias)


def _sb_kernel(q_ref, k_ref, v_ref, o_ref, qm_ref, acc_ref, carry_ref, z_ref, sp_ref):
    qi = pl.program_id(1)
    tq = SB_Q_ROWS
    n_sub = q_ref.shape[0] // tq
    wide = SB_BACK_KEYS
    n_pairs = SB_HEADS // 2
    lane = lax.broadcasted_iota(jnp.int32, (1, LANES), 1)
    even_lanes = lane < SB_HEAD_DIM

    def neg_suffix_mat(width):
        row = lax.broadcasted_iota(jnp.int32, (width, width), 0)
        col = lax.broadcasted_iota(jnp.int32, (width, width), 1)
        return jnp.where(row >= col, -1.0, 0.0).astype(BF16)

    suffix_mats = {tq: neg_suffix_mat(tq), wide: neg_suffix_mat(wide)}

    for s in range(n_sub):
        for p in range(n_pairs):
            qp = q_ref[s * tq:(s + 1) * tq, p * LANES:(p + 1) * LANES].astype(F32)
            qm_ref[s, p, 0:tq, :] = jnp.where(even_lanes, qp, 0.0).astype(BF16)
            qm_ref[s, p, tq:2 * tq, :] = jnp.where(even_lanes, 0.0, qp).astype(BF16)

    def visit_blocks(work, fresh):
        col0 = [0] * n_sub
        placed = []
        for s, key_start, width, diagonal in work:
            cols = slice(col0[s], col0[s] + width)
            strict = None
            if diagonal:
                r2 = lax.broadcasted_iota(jnp.int32, (2 * tq, width), 0)
                c2 = lax.broadcasted_iota(jnp.int32, (2 * tq, width), 1)
                strict = c2 < jnp.bitwise_and(r2, tq - 1)
            for p in range(n_pairs):
                kp = k_ref[pl.ds(key_start, width), p * LANES:(p + 1) * LANES]
                z = _dot_nt(qm_ref[s, p], kp)
                sp = _softplus(z)
                z_ref[s, p, :, cols] = z
                if diagonal:
                    sp = jnp.where(strict, sp, 0.0)
                sp_ref[s, p, :, cols] = sp.astype(BF16)
            placed.append((s, key_start, width, strict, cols, fresh and col0[s] == 0))
            col0[s] += width
        for s, key_start, width, strict, cols, first in placed:
            for p in range(n_pairs):
                pcols = slice(p * LANES, (p + 1) * LANES)
                incl = _dot(sp_ref[s, p, :, cols], suffix_mats[width])
                log_w = z_ref[s, p, :, cols] + incl
                if not first:
                    carry = carry_ref[s, p]
                    log_w = log_w + jnp.concatenate([carry] * (width // LANES), axis=1)
                if strict is not None:
                    log_w = jnp.where(strict, log_w, SB_MASKED_LOG_WEIGHT)
                w = jnp.exp(log_w).astype(BF16)
                d = _dot(w, v_ref[pl.ds(key_start, width), pcols])
                pv = jnp.where(even_lanes, d[0:tq], d[tq:2 * tq])
                block_total = incl[:, 0:1]
                if first:
                    acc_ref[s, :, pcols] = pv
                    carry_ref[s, p] = jnp.broadcast_to(block_total, (2 * tq, LANES))
                else:
                    acc_ref[s, :, pcols] += pv
                    carry_ref[s, p] = carry + block_total

    def any_row_alive(s):
        m = carry_ref[s, 0]
        for p in range(1, n_pairs):
            m = jnp.maximum(m, carry_ref[s, p])
        return (jnp.max(m) > SB_DEAD_LOG_WEIGHT).astype(jnp.int32)

    tile_index = [qi * n_sub + s for s in range(n_sub)]
    q0 = [pl.multiple_of(t * tq, tq) for t in tile_index]
    wide_tiles = wide // tq
    all_wide = qi * n_sub >= wide_tiles

    def first_visits(tiles_with_wide):
        work = []
        for s in range(n_sub):
            work.append((s, q0[s], tq, True))
            if s in tiles_with_wide:
                work.append((s, pl.multiple_of(q0[s] - wide, tq), wide, False))
        visit_blocks(work, fresh=True)

    @pl.when(all_wide)
    def _():
        first_visits(range(n_sub))

    @pl.when(jnp.logical_not(all_wide))
    def _():
        first_visits(range(wide_tiles, n_sub))

    def cond(state):
        kb, alive = state
        return jnp.logical_and(kb >= 0, alive > 0)

    alive = [any_row_alive(s) for s in range(n_sub)]
    for s in range(n_sub):
        def body(state, s=s):
            kb, _ = state
            visit_blocks([(s, pl.multiple_of(kb * tq, tq), tq, False)], fresh=False)
            return kb - 1, any_row_alive(s)

        next_block = jnp.where(tile_index[s] >= wide_tiles, tile_index[s] - wide_tiles, tile_index[s]) - 1
        lax.while_loop(cond, body, (next_block, alive[s]))
        o_ref[s * tq:(s + 1) * tq, :] = acc_ref[s].astype(o_ref.dtype)


def _stick_breaking(q, k, v, batch, seq):
    tq = SB_Q_ROWS
    n_sub = SB_TILES_PER_STEP
    assert n_sub >= SB_BACK_KEYS // tq
    n_q = seq // (tq * n_sub)
    n_pairs = SB_HEADS // 2
    return pl.pallas_call(
        _sb_kernel,
        out_shape=jax.ShapeDtypeStruct(q.shape, BF16),
        grid=(batch, n_q),
        in_specs=[
            pl.BlockSpec((n_sub * tq, SB_WIDTH), lambda b, i: (b * n_q + i, 0)),
            pl.BlockSpec((seq, SB_WIDTH), lambda b, i: (b, 0)),
            pl.BlockSpec((seq, SB_WIDTH), lambda b, i: (b, 0)),
        ],
        out_specs=pl.BlockSpec((n_sub * tq, SB_WIDTH), lambda b, i: (b * n_q + i, 0)),
        scratch_shapes=[
            pltpu.VMEM((n_sub, n_pairs, 2 * tq, LANES), BF16),
            pltpu.VMEM((n_sub, tq, SB_WIDTH), F32),
            pltpu.VMEM((n_sub, n_pairs, 2 * tq, LANES), F32),
            pltpu.VMEM((n_sub, n_pairs, 2 * tq, tq + SB_BACK_KEYS), F32),
            pltpu.VMEM((n_sub, n_pairs, 2 * tq, tq + SB_BACK_KEYS), BF16),
        ],
        compiler_params=pltpu.CompilerParams(
            dimension_semantics=("parallel", "parallel"), vmem_limit_bytes=VMEM_LIMIT_BYTES),
        name="stick_breaking_attention",
    )(q, k, v)


def _gla_kernel(q_ref, k_ref, v_ref, gg_ref, la_ref, ng_ref, o_ref, st_ref):
    @pl.when(pl.program_id(0) == 0)
    def _():
        st_ref[...] = jnp.zeros_like(st_ref)

    ch = GLA_CHUNK
    n_batch, n_rows = q_ref.shape[0], q_ref.shape[1]
    n_ch = n_rows // ch
    n_pairs = GLA_HEADS // 2
    shift = ch.bit_length() - 1
    r = lax.broadcasted_iota(jnp.int32, (n_rows, n_rows), 0)
    c = lax.broadcasted_iota(jnp.int32, (n_rows, n_rows), 1)
    same_chunk = jnp.right_shift(r, shift) == jnp.right_shift(c, shift)
    gate_mats = jnp.concatenate([
        jnp.where(jnp.logical_and(same_chunk, c <= r), 1.0, 0.0),
        jnp.where(jnp.logical_and(same_chunk, jnp.bitwise_and(c, ch - 1) < ch // 2), 1.0, 0.0),
        jnp.where(same_chunk, 1.0, 0.0)], axis=0).astype(BF16)
    r2 = lax.broadcasted_iota(jnp.int32, (2 * ch, 2 * ch), 0)
    c2 = lax.broadcasted_iota(jnp.int32, (2 * ch, 2 * ch), 1)
    same_head_causal = jnp.logical_and(jnp.bitwise_and(r2, ch) == jnp.bitwise_and(c2, ch),
                                       jnp.bitwise_and(c2, ch - 1) <= jnp.bitwise_and(r2, ch - 1))
    lane = lax.broadcasted_iota(jnp.int32, (1, LANES), 1)
    even_lanes = lane < GLA_KEY_DIM
    norm_g = ng_ref[...]

    def stack_masked(t):
        return jnp.concatenate([jnp.where(even_lanes, t, 0.0), jnp.where(even_lanes, 0.0, t)],
                               axis=0).astype(BF16)

    def stack_heads(t):
        return jnp.concatenate([t[:, :GLA_VAL_DIM], t[:, GLA_VAL_DIM:]], axis=0)

    bodies = [(bi, ci, p) for bi in range(n_batch) for ci in range(n_ch) for p in range(n_pairs)]
    rows_of = lambda ci: slice(ci * ch, (ci + 1) * ch)
    pair_of = lambda p: slice(p * LANES, (p + 1) * LANES)
    vcols_of = lambda p: slice(2 * p * GLA_VAL_DIM, (2 * p + 2) * GLA_VAL_DIM)

    gated = []
    for bi in range(n_batch):
        g_hi, g_lo = _split_bf16(la_ref[bi])
        b3 = _dot(gate_mats, g_hi) + _dot(gate_mats, g_lo)
        b, b_mid, b_last = b3[0:n_rows], b3[n_rows:2 * n_rows], b3[2 * n_rows:]
        q = q_ref[bi].astype(F32)
        k = k_ref[bi].astype(F32)
        gated.append(dict(q_in=q * jnp.exp(b - b_mid), k_in=(k * jnp.exp(b_mid - b)).astype(BF16),
                          k_dec=k * jnp.exp(b_last - b), q_b=q * jnp.exp(b), b_last=b_last))

    v_st, scores = {}, {}
    for bi, ci, p in bodies:
        rows, pair = rows_of(ci), pair_of(p)
        v_st[bi, ci, p] = stack_heads(v_ref[bi, rows, vcols_of(p)])
        k_pair = gated[bi]["k_in"][rows, pair]
        s = _dot_nt(stack_masked(gated[bi]["q_in"][rows, pair]), jnp.concatenate([k_pair, k_pair], axis=0))
        scores[bi, ci, p] = jnp.where(same_head_causal, s, 0.0).astype(BF16)

    o_intra, upd = {}, {}
    for body in bodies:
        bi, ci, p = body
        o_intra[body] = _dot(scores[body], v_st[body])
        upd[body] = _dot(v_st[body].astype(F32).T.astype(BF16),
                         stack_masked(gated[bi]["k_dec"][rows_of(ci), pair_of(p)]))

    st_in = {}
    for bi in range(n_batch):
        for p in range(n_pairs):
            st = st_ref[bi, p]
            for ci in range(n_ch):
                st_in[bi, ci, p] = st
                decay = jnp.exp(gated[bi]["b_last"][ci * ch:ci * ch + 1, pair_of(p)])
                st = st * decay + upd[bi, ci, p]
            st_ref[bi, p] = st

    for body in bodies:
        bi, ci, p = body
        rows = rows_of(ci)
        o = o_intra[body] + _dot_nt(stack_masked(gated[bi]["q_b"][rows, pair_of(p)]),
                                    st_in[body].astype(BF16))
        on = o * lax.rsqrt(jnp.mean(o * o, axis=-1, keepdims=True) + RMS_EPS) * norm_g
        gate = stack_heads(gg_ref[bi, rows, vcols_of(p)].astype(F32))
        res = (on * (gate / (1.0 + jnp.exp(-gate)))).astype(o_ref.dtype)
        o_ref[bi, rows, vcols_of(p)] = jnp.concatenate([res[:ch], res[ch:]], axis=1)


def _gla(gq, gk, gv, gg, log_a, norm_g, batch, seq):
    rows = GLA_ROWS
    shaped = lambda t: t.reshape(batch, seq, t.shape[-1])
    tile = lambda i: (0, i, 0)
    out = pl.pallas_call(
        _gla_kernel,
        out_shape=jax.ShapeDtypeStruct((batch, seq, GLA_WIDTH), BF16),
        grid=(seq // rows,),
        in_specs=[
            pl.BlockSpec((batch, rows, GLA_KEYS), tile),
            pl.BlockSpec((batch, rows, GLA_KEYS), tile),
            pl.BlockSpec((batch, rows, GLA_WIDTH), tile),
            pl.BlockSpec((batch, rows, GLA_WIDTH), tile),
            pl.BlockSpec((batch, rows, GLA_KEYS), tile),
            pl.BlockSpec((1, GLA_VAL_DIM), lambda i: (0, 0)),
        ],
        out_specs=pl.BlockSpec((batch, rows, GLA_WIDTH), tile),
        scratch_shapes=[pltpu.VMEM((batch, GLA_HEADS // 2, GLA_VAL_DIM, LANES), F32)],
        compiler_params=pltpu.CompilerParams(
            dimension_semantics=("arbitrary",), vmem_limit_bytes=VMEM_LIMIT_BYTES),
        name="gated_linear_attention",
    )(shaped(gq), shaped(gk), shaped(gv), shaped(gg), shaped(log_a), norm_g)
    return out.reshape(batch * seq, GLA_WIDTH)


def _mix_ffn_kernel(alpha, tiles_per_seq, sb_ref, gla_ref, x_ref, wo_ref, g1_ref, b1_ref,
                    wup_ref, cw_ref, cb_ref, wd_ref, g2_ref, b2_ref, o_ref,
                    h_ref, ua_ref, uc_ref, act_ref):
    rows = x_ref.shape[0]
    n_chunks = D_FF // FFN_CHUNK

    @pl.when(pl.program_id(0) % tiles_per_seq == 0)
    def _():
        ua_ref[:, 0:CONV_TAIL, :] = jnp.zeros((n_chunks, CONV_TAIL, FFN_CHUNK), F32)
        uc_ref[:, 0:CONV_TAIL, :] = jnp.zeros((n_chunks, CONV_TAIL, FFN_CHUNK), F32)

    mix = _dot(sb_ref[...], wo_ref[0:SB_WIDTH, :]) + _dot(gla_ref[...], wo_ref[SB_WIDTH:, :])
    h = _layer_norm(alpha * x_ref[...] + mix, g1_ref[...], b1_ref[...])
    h_ref[...] = h
    hb = h.astype(BF16)

    def conv(u_ref, slot, c0, scale):
        w = cw_ref[:, c0:c0 + FFN_CHUNK] * scale
        return (w[2:3, :] * u_ref[slot, CONV_TAIL:CONV_TAIL + rows, :]
                + w[1:2, :] * u_ref[slot, CONV_TAIL - 1:CONV_TAIL - 1 + rows, :]
                + w[0:1, :] * u_ref[slot, CONV_TAIL - 2:CONV_TAIL - 2 + rows, :]
                + cb_ref[:, c0:c0 + FFN_CHUNK] * scale)

    for ci in range(n_chunks):
        a0 = ci * FFN_CHUNK
        c0 = D_FF + ci * FFN_CHUNK
        ua_ref[ci, CONV_TAIL:, :] = _dot(hb, wup_ref[:, a0:a0 + FFN_CHUNK])
        uc_ref[ci, CONV_TAIL:, :] = _dot(hb, wup_ref[:, c0:c0 + FFN_CHUNK])
        half_a = conv(ua_ref, ci, a0, 0.5)
        c = conv(uc_ref, ci, c0, 1.0)
        act_ref[:, a0:a0 + FFN_CHUNK] = (
            half_a * (1.0 + lax.erf(half_a * (2.0 ** 0.5))) * c).astype(BF16)
        for u_ref in (ua_ref, uc_ref):
            u_ref[ci, 0:CONV_TAIL, :] = u_ref[ci, rows:rows + CONV_TAIL, :]

    f = _dot(act_ref[...], wd_ref[...])
    o_ref[...] = _layer_norm(alpha * h_ref[...] + f, g2_ref[...], b2_ref[...])


def _mix_ffn(sb_o, gla_o, x, w_out, g1, b1, w_up, conv_w, conv_b, w_down, g2, b2, alpha, seq):
    tokens = x.shape[0]
    rows = FFN_ROWS
    n_chunks = D_FF // FFN_CHUNK
    assert n_chunks * FFN_CHUNK == D_FF and seq % rows == 0
    tile = lambda i: (i, 0)
    const = lambda i: (0, 0)
    whole = lambda t: pl.BlockSpec(t.shape, const)
    return pl.pallas_call(
        functools.partial(_mix_ffn_kernel, alpha, seq // rows),
        out_shape=jax.ShapeDtypeStruct((tokens, D_MODEL), F32),
        grid=(tokens // rows,),
        in_specs=[
            pl.BlockSpec((rows, SB_WIDTH), tile),
            pl.BlockSpec((rows, GLA_WIDTH), tile),
            pl.BlockSpec((rows, D_MODEL), tile),
            whole(w_out), whole(g1), whole(b1),
            whole(w_up), whole(conv_w), whole(conv_b), whole(w_down), whole(g2), whole(b2),
        ],
        out_specs=pl.BlockSpec((rows, D_MODEL), tile),
        scratch_shapes=[
            pltpu.VMEM((rows, D_MODEL), F32),
            pltpu.VMEM((n_chunks, CONV_TAIL + rows, FFN_CHUNK), F32),
            pltpu.VMEM((n_chunks, CONV_TAIL + rows, FFN_CHUNK), F32),
            pltpu.VMEM((rows, D_FF), BF16),
        ],
        compiler_params=pltpu.CompilerParams(
            dimension_semantics=("arbitrary",), vmem_limit_bytes=VMEM_LIMIT_BYTES),
        name="mix_ffn",
    )(sb_o, gla_o, x, w_out, g1, b1, w_up, conv_w, conv_b, w_down, g2, b2)


def _layer(x, w_in, layer, gate_up, gate_bias, gla_norm_g, w_out, ln1_g, ln1_b,
           w_up, conv_w, conv_b, w_down, ln2_g, ln2_b, alpha, batch, seq):
    row = lambda t: t.reshape(1, -1)
    q, k, v, gq, gk, gv, gg, log_a = _input_projection(
        x, w_in, layer, gate_up, row(gate_bias))
    sb_o = _stick_breaking(q, k, v, batch, seq)
    gla_o = _gla(gq, gk, gv, gg, log_a, row(gla_norm_g), batch, seq)
    return _mix_ffn(sb_o, gla_o, x, w_out.astype(BF16), row(ln1_g), row(ln1_b),
                    w_up.astype(BF16), conv_w, row(conv_b), w_down.astype(BF16),
                    row(ln2_g), row(ln2_b), alpha, seq)


def kernel(x, w_in, gate_up, gate_bias, gla_norm_g, w_out, ln1_g, ln1_b, w_up, conv_w, conv_b,
           w_down, ln2_g, ln2_b):
    batch, seq, d_model = x.shape
    depth = w_in.shape[0]
    alpha = (2.0 * depth) ** 0.25
    y = x.reshape(batch * seq, d_model)
    for l in range(depth):
        y = _layer(y, w_in, l, gate_up[l], gate_bias[l], gla_norm_g[l], w_out[l], ln1_g[l], ln1_b[l],
                   w_up[l], conv_w[l], conv_b[l], w_down[l], ln2_g[l], ln2_b[l], alpha, batch, seq)
    return y.reshape(batch, seq, d_model)
```

```python
import functools

import jax
import jax.numpy as jnp
from jax import lax
from jax.experimental import pallas as pl
from jax.experimental.pallas import tpu as pltpu

F32 = jnp.float32
BF16 = jnp.bfloat16

D_MODEL = 1024
SB_WIDTH = 512
SB_HEADS = 8
SB_HEAD_DIM = 64
GLA_WIDTH = 512
GLA_HEADS = 4
GLA_KEY_DIM = 64
GLA_VAL_DIM = 128
GLA_KEYS = GLA_HEADS * GLA_KEY_DIM
GLA_GATE_RANK = 16
GLA_TAU = 16.0
GLA_CHUNK = 64
D_FF = 2816
CONV_WIDTH = 3
LN_EPS = 1e-5
RMS_EPS = 1e-6
MAIN_WIDTH = 3 * SB_WIDTH + 2 * GLA_KEYS + 2 * GLA_WIDTH

LANES = 128
SUBLANES = 8
MXU_WIDTH = 256
VMEM_LIMIT_BYTES = 56 * 1024 * 1024

PROJ_ROWS = 1024
SB_Q_ROWS = 128
SB_BACK_KEYS = 256
SB_TILES_PER_STEP = 4
GLA_ROWS = 256
FFN_ROWS = 512
FFN_CHUNK = MXU_WIDTH
CONV_TAIL = SUBLANES

SB_DEAD_LOG_WEIGHT = -90.0
SB_MASKED_LOG_WEIGHT = -1e30
F32_SIGN_BIT = -2 ** 31

NT_DIMS = (((1,), (1,)), ((), ()))


def _split_bf16(x):
    hi = x.astype(BF16)
    lo = (x - hi.astype(F32)).astype(BF16)
    return hi, lo


def _dot(a, b):
    return jnp.dot(a, b, preferred_element_type=F32)


def _dot_nt(a, b):
    return lax.dot_general(a, b, NT_DIMS, preferred_element_type=F32)


def _softplus(z):
    neg_abs = lax.bitcast_convert_type(
        lax.bitcast_convert_type(z, jnp.int32) | jnp.int32(F32_SIGN_BIT), F32)
    return jnp.maximum(z, 0.0) + jnp.log(1.0 + jnp.exp(neg_abs))


def _layer_norm(y, g, b):
    mu = jnp.mean(y, axis=-1, keepdims=True)
    d = y - mu
    var = jnp.mean(d * d, axis=-1, keepdims=True)
    return d * lax.rsqrt(var + LN_EPS) * g + b


def _proj_kernel(x_ref, w32_ref, gup_ref, gbias_ref,
                 q_ref, k_ref, v_ref, gq_ref, gk_ref, gv_ref, gg_ref, la_ref, w_ref):
    @pl.when(pl.program_id(0) == 0)
    def _():
        w_ref[...] = w32_ref[...].astype(BF16)

    xb = x_ref[...].astype(BF16)
    col = 0
    for out_ref, scale in ((q_ref, SB_HEAD_DIM ** -0.5), (k_ref, None), (v_ref, None),
                           (gq_ref, GLA_KEY_DIM ** -0.5), (gk_ref, None), (gv_ref, None),
                           (gg_ref, None)):
        width = out_ref.shape[1]
        y = _dot_nt(xb, w_ref[col:col + width, :])
        if scale is not None:
            y = y * scale
        out_ref[...] = y.astype(out_ref.dtype)
        col += width
    ga = _dot_nt(xb, w_ref[col:col + GLA_GATE_RANK, :])
    logits = _dot(ga.astype(BF16), gup_ref[...].astype(BF16)) + gbias_ref[...]
    la_ref[...] = -_softplus(-logits) * (1.0 / GLA_TAU)


def _input_projection(x, w_in, layer, gate_up, gate_bias):
    w_t = jnp.swapaxes(w_in, 1, 2)
    in_width = MAIN_WIDTH + GLA_GATE_RANK
    tokens = x.shape[0]
    rows = PROJ_ROWS
    widths = (SB_WIDTH, SB_WIDTH, SB_WIDTH, GLA_KEYS, GLA_KEYS, GLA_WIDTH, GLA_WIDTH)
    const = lambda i: (0, 0)
    tile = lambda i: (i, 0)
    out_shape = [jax.ShapeDtypeStruct((tokens, w), BF16) for w in widths]
    out_shape.append(jax.ShapeDtypeStruct((tokens, GLA_KEYS), F32))
    out_specs = [pl.BlockSpec((rows, w), tile) for w in widths]
    out_specs.append(pl.BlockSpec((rows, GLA_KEYS), tile))
    return pl.pallas_call(
        _proj_kernel,
        out_shape=out_shape,
        grid=(tokens // rows,),
        in_specs=[
            pl.BlockSpec((rows, D_MODEL), tile),
            pl.BlockSpec((None, in_width, D_MODEL), lambda i: (layer, 0, 0)),
            pl.BlockSpec((GLA_GATE_RANK, GLA_KEYS), const),
            pl.BlockSpec((1, GLA_KEYS), const),
        ],
        out_specs=out_specs,
        scratch_shapes=[pltpu.VMEM((in_width, D_MODEL), BF16)],
        compiler_params=pltpu.CompilerParams(
            dimension_semantics=("arbitrary",), vmem_limit_bytes=VMEM_LIMIT_BYTES),
        name="input_projection",
    )(x, w_t, gate_up, gate_bias)


def _sb_kernel(q_ref, k_ref, v_ref, o_ref, qm_ref, acc_ref, carry_ref, z_ref, sp_ref):
    qi = pl.program_id(1)
    tq = SB_Q_ROWS
    n_sub = q_ref.shape[0] // tq
    wide = SB_BACK_KEYS
    n_pairs = SB_HEADS // 2
    lane = lax.broadcasted_iota(jnp.int32, (1, LANES), 1)
    even_lanes = lane < SB_HEAD_DIM

    def neg_suffix_mat(width):
        row = lax.broadcasted_iota(jnp.int32, (width, width), 0)
        col = lax.broadcasted_iota(jnp.int32, (width, width), 1)
        return jnp.where(row >= col, -1.0, 0.0).astype(BF16)

    suffix_mats = {tq: neg_suffix_mat(tq), wide: neg_suffix_mat(wide)}

    for s in range(n_sub):
        for p in range(n_pairs):
            qp = q_ref[s * tq:(s + 1) * tq, p * LANES:(p + 1) * LANES].astype(F32)
            qm_ref[s, p, 0:tq, :] = jnp.where(even_lanes, qp, 0.0).astype(BF16)
            qm_ref[s, p, tq:2 * tq, :] = jnp.where(even_lanes, 0.0, qp).astype(BF16)

    def visit_blocks(work, fresh):
        col0 = [0] * n_sub
        placed = []
        for s, key_start, width, diagonal in work:
            cols = slice(col0[s], col0[s] + width)
            strict = None
            if diagonal:
                r2 = lax.broadcasted_iota(jnp.int32, (2 * tq, width), 0)
                c2 = lax.broadcasted_iota(jnp.int32, (2 * tq, width), 1)
                strict = c2 < jnp.bitwise_and(r2, tq - 1)
            for p in range(n_pairs):
                kp = k_ref[pl.ds(key_start, width), p * LANES:(p + 1) * LANES]
                z = _dot_nt(qm_ref[s, p], kp)
                sp = _softplus(z)
                z_ref[s, p, :, cols] = z
                if diagonal:
                    sp = jnp.where(strict, sp, 0.0)
                sp_ref[s, p, :, cols] = sp.astype(BF16)
            placed.append((s, key_start, width, strict, cols, fresh and col0[s] == 0))
            col0[s] += width
        for s, key_start, width, strict, cols, first in placed:
            for p in range(n_pairs):
                pcols = slice(p * LANES, (p + 1) * LANES)
                incl = _dot(sp_ref[s, p, :, cols], suffix_mats[width])
                log_w = z_ref[s, p, :, cols] + incl
                if not first:
                    carry = carry_ref[s, p]
                    log_w = log_w + jnp.concatenate([carry] * (width // LANES), axis=1)
                if strict is not None:
                    log_w = jnp.where(strict, log_w, SB_MASKED_LOG_WEIGHT)
                w = jnp.exp(log_w).astype(BF16)
                d = _dot(w, v_ref[pl.ds(key_start, width), pcols])
                pv = jnp.where(even_lanes, d[0:tq], d[tq:2 * tq])
                block_total = incl[:, 0:1]
                if first:
                    acc_ref[s, :, pcols] = pv
                    carry_ref[s, p] = jnp.broadcast_to(block_total, (2 * tq, LANES))
                else:
                    acc_ref[s, :, pcols] += pv
                    carry_ref[s, p] = carry + block_total

    def any_row_alive(s):
        m = carry_ref[s, 0]
        for p in range(1, n_pairs):
            m = jnp.maximum(m, carry_ref[s, p])
        return (jnp.max(m) > SB_DEAD_LOG_WEIGHT).astype(jnp.int32)

    tile_index = [qi * n_sub + s for s in range(n_sub)]
    q0 = [pl.multiple_of(t * tq, tq) for t in tile_index]
    wide_tiles = wide // tq
    all_wide = qi * n_sub >= wide_tiles

    def first_visits(tiles_with_wide):
        work = []
        for s in range(n_sub):
            work.append((s, q0[s], tq, True))
            if s in tiles_with_wide:
                work.append((s, pl.multiple_of(q0[s] - wide, tq), wide, False))
        visit_blocks(work, fresh=True)

    @pl.when(all_wide)
    def _():
        first_visits(range(n_sub))

    @pl.when(jnp.logical_not(all_wide))
    def _():
        first_visits(range(wide_tiles, n_sub))

    def cond(state):
        kb, alive = state
        return jnp.logical_and(kb >= 0, alive > 0)

    alive = [any_row_alive(s) for s in range(n_sub)]
    for s in range(n_sub):
        def body(state, s=s):
            kb, _ = state
            visit_blocks([(s, pl.multiple_of(kb * tq, tq), tq, False)], fresh=False)
            return kb - 1, any_row_alive(s)

        next_block = jnp.where(tile_index[s] >= wide_tiles, tile_index[s] - wide_tiles, tile_index[s]) - 1
        lax.while_loop(cond, body, (next_block, alive[s]))
        o_ref[s * tq:(s + 1) * tq, :] = acc_ref[s].astype(o_ref.dtype)


def _stick_breaking(q, k, v, batch, seq):
    tq = SB_Q_ROWS
    n_sub = SB_TILES_PER_STEP
    assert n_sub >= SB_BACK_KEYS // tq
    n_q = seq // (tq * n_sub)
    n_pairs = SB_HEADS // 2
    return pl.pallas_call(
        _sb_kernel,
        out_shape=jax.ShapeDtypeStruct(q.shape, BF16),
        grid=(batch, n_q),
        in_specs=[
            pl.BlockSpec((n_sub * tq, SB_WIDTH), lambda b, i: (b * n_q + i, 0)),
            pl.BlockSpec((seq, SB_WIDTH), lambda b, i: (b, 0)),
            pl.BlockSpec((seq, SB_WIDTH), lambda b, i: (b, 0)),
        ],
        out_specs=pl.BlockSpec((n_sub * tq, SB_WIDTH), lambda b, i: (b * n_q + i, 0)),
        scratch_shapes=[
            pltpu.VMEM((n_sub, n_pairs, 2 * tq, LANES), BF16),
            pltpu.VMEM((n_sub, tq, SB_WIDTH), F32),
            pltpu.VMEM((n_sub, n_pairs, 2 * tq, LANES), F32),
            pltpu.VMEM((n_sub, n_pairs, 2 * tq, tq + SB_BACK_KEYS), F32),
            pltpu.VMEM((n_sub, n_pairs, 2 * tq, tq + SB_BACK_KEYS), BF16),
        ],
        compiler_params=pltpu.CompilerParams(
            dimension_semantics=("parallel", "parallel"), vmem_limit_bytes=VMEM_LIMIT_BYTES),
        name="stick_breaking_attention",
    )(q, k, v)


def _gla_kernel(q_ref, k_ref, v_ref, gg_ref, la_ref, ng_ref, o_ref, st_ref):
    @pl.when(pl.program_id(0) == 0)
    def _():
        st_ref[...] = jnp.zeros_like(st_ref)

    ch = GLA_CHUNK
    n_batch, n_rows = q_ref.shape[0], q_ref.shape[1]
    n_ch = n_rows // ch
    n_pairs = GLA_HEADS // 2
    shift = ch.bit_length() - 1
    r = lax.broadcasted_iota(jnp.int32, (n_rows, n_rows), 0)
    c = lax.broadcasted_iota(jnp.int32, (n_rows, n_rows), 1)
    same_chunk = jnp.right_shift(r, shift) == jnp.right_shift(c, shift)
    gate_mats = jnp.concatenate([
        jnp.where(jnp.logical_and(same_chunk, c <= r), 1.0, 0.0),
        jnp.where(jnp.logical_and(same_chunk, jnp.bitwise_and(c, ch - 1) < ch // 2), 1.0, 0.0),
        jnp.where(same_chunk, 1.0, 0.0)], axis=0).astype(BF16)
    r2 = lax.broadcasted_iota(jnp.int32, (2 * ch, 2 * ch), 0)
    c2 = lax.broadcasted_iota(jnp.int32, (2 * ch, 2 * ch), 1)
    same_head_causal = jnp.logical_and(jnp.bitwise_and(r2, ch) == jnp.bitwise_and(c2, ch),
                                       jnp.bitwise_and(c2, ch - 1) <= jnp.bitwise_and(r2, ch - 1))
    lane = lax.broadcasted_iota(jnp.int32, (1, LANES), 1)
    even_lanes = lane < GLA_KEY_DIM
    norm_g = ng_ref[...]

    def stack_masked(t):
        return jnp.concatenate([jnp.where(even_lanes, t, 0.0), jnp.where(even_lanes, 0.0, t)],
                               axis=0).astype(BF16)

    def stack_heads(t):
        return jnp.concatenate([t[:, :GLA_VAL_DIM], t[:, GLA_VAL_DIM:]], axis=0)

    bodies = [(bi, ci, p) for bi in range(n_batch) for ci in range(n_ch) for p in range(n_pairs)]
    rows_of = lambda ci: slice(ci * ch, (ci + 1) * ch)
    pair_of = lambda p: slice(p * LANES, (p + 1) * LANES)
    vcols_of = lambda p: slice(2 * p * GLA_VAL_DIM, (2 * p + 2) * GLA_VAL_DIM)

    gated = []
    for bi in range(n_batch):
        g_hi, g_lo = _split_bf16(la_ref[bi])
        b3 = _dot(gate_mats, g_hi) + _dot(gate_mats, g_lo)
        b, b_mid, b_last = b3[0:n_rows], b3[n_rows:2 * n_rows], b3[2 * n_rows:]
        q = q_ref[bi].astype(F32)
        k = k_ref[bi].astype(F32)
        k_in = k * jnp.exp(b_mid - b)
        gated.append(dict(q_in=q * jnp.exp(b - b_mid), k_in=k_in, k_in_b=k_in.astype(BF16),
                          b_mid=b_mid, b_last=b_last))

    v_st, scores = {}, {}
    for bi, ci, p in bodies:
        rows, pair = rows_of(ci), pair_of(p)
        v_st[bi, ci, p] = stack_heads(v_ref[bi, rows, vcols_of(p)])
        k_pair = gated[bi]["k_in_b"][rows, pair]
        s = _dot_nt(stack_masked(gated[bi]["q_in"][rows, pair]), jnp.concatenate([k_pair, k_pair], axis=0))
        scores[bi, ci, p] = jnp.where(same_head_causal, s, 0.0).astype(BF16)

    o_intra, upd = {}, {}
    for body in bodies:
        bi, ci, p = body
        o_intra[body] = _dot(scores[body], v_st[body])
        row, pair = slice(ci * ch, ci * ch + 1), pair_of(p)
        k_dec = gated[bi]["k_in"][rows_of(ci), pair] * jnp.exp(
            gated[bi]["b_last"][row, pair] - gated[bi]["b_mid"][row, pair])
        upd[body] = _dot(v_st[body].astype(F32).T.astype(BF16), stack_masked(k_dec))

    st_in = {}
    for bi in range(n_batch):
        for p in range(n_pairs):
            st = st_ref[bi, p]
            for ci in range(n_ch):
                st_in[bi, ci, p] = st
                decay = jnp.exp(gated[bi]["b_last"][ci * ch:ci * ch + 1, pair_of(p)])
                st = st * decay + upd[bi, ci, p]
            st_ref[bi, p] = st

    for body in bodies:
        bi, ci, p = body
        rows, pair = rows_of(ci), pair_of(p)
        q_b = gated[bi]["q_in"][rows, pair] * jnp.exp(gated[bi]["b_mid"][ci * ch:ci * ch + 1, pair])
        o = o_intra[body] + _dot_nt(stack_masked(q_b), st_in[body].astype(BF16))
        on = o * lax.rsqrt(jnp.mean(o * o, axis=-1, keepdims=True) + RMS_EPS) * norm_g
        gate = stack_heads(gg_ref[bi, rows, vcols_of(p)].astype(F32))
        res = (on * (gate / (1.0 + jnp.exp(-gate)))).astype(o_ref.dtype)
        o_ref[bi, rows, vcols_of(p)] = jnp.concatenate([res[:ch], res[ch:]], axis=1)


def _gla(gq, gk, gv, gg, log_a, norm_g, batch, seq):
    rows = GLA_ROWS
    shaped = lambda t: t.reshape(batch, seq, t.shape[-1])
    tile = lambda i: (0, i, 0)
    out = pl.pallas_call(
        _gla_kernel,
        out_shape=jax.ShapeDtypeStruct((batch, seq, GLA_WIDTH), BF16),
        grid=(seq // rows,),
        in_specs=[
            pl.BlockSpec((batch, rows, GLA_KEYS), tile),
            pl.BlockSpec((batch, rows, GLA_KEYS), tile),
            pl.BlockSpec((batch, rows, GLA_WIDTH), tile),
            pl.BlockSpec((batch, rows, GLA_WIDTH), tile),
            pl.BlockSpec((batch, rows, GLA_KEYS), tile),
            pl.BlockSpec((1, GLA_VAL_DIM), lambda i: (0, 0)),
        ],
        out_specs=pl.BlockSpec((batch, rows, GLA_WIDTH), tile),
        scratch_shapes=[pltpu.VMEM((batch, GLA_HEADS // 2, GLA_VAL_DIM, LANES), F32)],
        compiler_params=pltpu.CompilerParams(
            dimension_semantics=("arbitrary",), vmem_limit_bytes=VMEM_LIMIT_BYTES),
        name="gated_linear_attention",
    )(shaped(gq), shaped(gk), shaped(gv), shaped(gg), shaped(log_a), norm_g)
    return out.reshape(batch * seq, GLA_WIDTH)


def _mix_ffn_kernel(alpha, tiles_per_seq, sb_ref, gla_ref, x_ref, wo_ref, g1_ref, b1_ref,
                    wup_ref, cw_ref, cb_ref, wd_ref, g2_ref, b2_ref, o_ref,
                    h_ref, ua_ref, uc_ref, act_ref):
    rows = x_ref.shape[0]
    n_chunks = D_FF // FFN_CHUNK

    @pl.when(pl.program_id(0) % tiles_per_seq == 0)
    def _():
        ua_ref[:, 0:CONV_TAIL, :] = jnp.zeros((n_chunks, CONV_TAIL, FFN_CHUNK), F32)
        uc_ref[:, 0:CONV_TAIL, :] = jnp.zeros((n_chunks, CONV_TAIL, FFN_CHUNK), F32)

    mix = _dot(sb_ref[...], wo_ref[0:SB_WIDTH, :]) + _dot(gla_ref[...], wo_ref[SB_WIDTH:, :])
    h = _layer_norm(alpha * x_ref[...] + mix, g1_ref[...], b1_ref[...])
    h_ref[...] = h
    hb = h.astype(BF16)

    def conv(u_ref, slot, c0, scale):
        w = cw_ref[:, c0:c0 + FFN_CHUNK] * scale
        return (w[2:3, :] * u_ref[slot, CONV_TAIL:CONV_TAIL + rows, :]
                + w[1:2, :] * u_ref[slot, CONV_TAIL - 1:CONV_TAIL - 1 + rows, :]
                + w[0:1, :] * u_ref[slot, CONV_TAIL - 2:CONV_TAIL - 2 + rows, :]
                + cb_ref[:, c0:c0 + FFN_CHUNK] * scale)

    for ci in range(n_chunks):
        a0 = ci * FFN_CHUNK
        c0 = D_FF + ci * FFN_CHUNK
        ua_ref[ci, CONV_TAIL:, :] = _dot(hb, wup_ref[:, a0:a0 + FFN_CHUNK])
        uc_ref[ci, CONV_TAIL:, :] = _dot(hb, wup_ref[:, c0:c0 + FFN_CHUNK])
        half_a = conv(ua_ref, ci, a0, 0.5)
        c = conv(uc_ref, ci, c0, 1.0)
        act_ref[:, a0:a0 + FFN_CHUNK] = (
            half_a * (1.0 + lax.erf(half_a * (2.0 ** 0.5))) * c).astype(BF16)
        for u_ref in (ua_ref, uc_ref):
            u_ref[ci, 0:CONV_TAIL, :] = u_ref[ci, rows:rows + CONV_TAIL, :]

    f = _dot(act_ref[...], wd_ref[...])
    o_ref[...] = _layer_norm(alpha * h_ref[...] + f, g2_ref[...], b2_ref[...])


def _mix_ffn(sb_o, gla_o, x, w_out, g1, b1, w_up, conv_w, conv_b, w_down, g2, b2, alpha, seq):
    tokens = x.shape[0]
    rows = FFN_ROWS
    n_chunks = D_FF // FFN_CHUNK
    assert n_chunks * FFN_CHUNK == D_FF and seq % rows == 0
    tile = lambda i: (i, 0)
    const = lambda i: (0, 0)
    whole = lambda t: pl.BlockSpec(t.shape, const)
    return pl.pallas_call(
        functools.partial(_mix_ffn_kernel, alpha, seq // rows),
        out_shape=jax.ShapeDtypeStruct((tokens, D_MODEL), F32),
        grid=(tokens // rows,),
        in_specs=[
            pl.BlockSpec((rows, SB_WIDTH), tile),
            pl.BlockSpec((rows, GLA_WIDTH), tile),
            pl.BlockSpec((rows, D_MODEL), tile),
            whole(w_out), whole(g1), whole(b1),
            whole(w_up), whole(conv_w), whole(conv_b), whole(w_down), whole(g2), whole(b2),
        ],
        out_specs=pl.BlockSpec((rows, D_MODEL), tile),
        scratch_shapes=[
            pltpu.VMEM((rows, D_MODEL), F32),
            pltpu.VMEM((n_chunks, CONV_TAIL + rows, FFN_CHUNK), F32),
            pltpu.VMEM((n_chunks, CONV_TAIL + rows, FFN_CHUNK), F32),
            pltpu.VMEM((rows, D_FF), BF16),
        ],
        compiler_params=pltpu.CompilerParams(
            dimension_semantics=("arbitrary",), vmem_limit_bytes=VMEM_LIMIT_BYTES),
        name="mix_ffn",
    )(sb_o, gla_o, x, w_out, g1, b1, w_up, conv_w, conv_b, w_down, g2, b2)


def _layer(x, w_in, layer, gate_up, gate_bias, gla_norm_g, w_out, ln1_g, ln1_b,
           w_up, conv_w, conv_b, w_down, ln2_g, ln2_b, alpha, batch, seq):
    row = lambda t: t.reshape(1, -1)
    q, k, v, gq, gk, gv, gg, log_a = _input_projection(
        x, w_in, layer, gate_up, row(gate_bias))
    sb_o = _stick_breaking(q, k, v, batch, seq)
    gla_o = _gla(gq, gk, gv, gg, log_a, row(gla_norm_g), batch, seq)
    return _mix_ffn(sb_o, gla_o, x, w_out.astype(BF16), row(ln1_g), row(ln1_b),
                    w_up.astype(BF16), conv_w, row(conv_b), w_down.astype(BF16),
                    row(ln2_g), row(ln2_b), alpha, seq)


def kernel(x, w_in, gate_up, gate_bias, gla_norm_g, w_out, ln1_g, ln1_b, w_up, conv_w, conv_b,
           w_down, ln2_g, ln2_b):
    batch, seq, d_model = x.shape
    depth = w_in.shape[0]
    alpha = (2.0 * depth) ** 0.25
    y = x.reshape(batch * seq, d_model)
    for l in range(depth):
        y = _layer(y, w_in, l, gate_up[l], gate_bias[l], gla_norm_g[l], w_out[l], ln1_g[l], ln1_b[l],
                   w_up[l], conv_w[l], conv_b[l], w_down[l], ln2_g[l], ln2_b[l], alpha, batch, seq)
    return y.reshape(batch, seq, d_model)
```

```python
import functools

import jax
import jax.numpy as jnp
from jax import lax
from jax.experimental import pallas as pl
from jax.experimental.pallas import tpu as pltpu

F32 = jnp.float32
BF16 = jnp.bfloat16

D_MODEL = 1024
SB_WIDTH = 512
SB_HEADS = 8
SB_HEAD_DIM = 64
GLA_WIDTH = 512
GLA_HEADS = 4
GLA_KEY_DIM = 64
GLA_VAL_DIM = 128
GLA_KEYS = GLA_HEADS * GLA_KEY_DIM
GLA_GATE_RANK = 16
GLA_TAU = 16.0
GLA_CHUNK = 64
D_FF = 2816
CONV_WIDTH = 3
LN_EPS = 1e-5
RMS_EPS = 1e-6
MAIN_WIDTH = 3 * SB_WIDTH + 2 * GLA_KEYS + 2 * GLA_WIDTH

LANES = 128
SUBLANES = 8
MXU_WIDTH = 256
VMEM_LIMIT_BYTES = 56 * 1024 * 1024

PROJ_ROWS = 1024
SB_Q_ROWS = 64
SB_WINDOW = MXU_WIDTH
SB_TILES_PER_STEP = 8
GLA_ROWS = 256
FFN_ROWS = 512
FFN_CHUNK = MXU_WIDTH
CONV_TAIL = SUBLANES

SB_DEAD_LOG_WEIGHT = -90.0
SB_MASKED_LOG_WEIGHT = -1e30
F32_SIGN_BIT = -2 ** 31

NT_DIMS = (((1,), (1,)), ((), ()))


def _split_bf16(x):
    hi = x.astype(BF16)
    lo = (x - hi.astype(F32)).astype(BF16)
    return hi, lo


def _dot(a, b):
    return jnp.dot(a, b, preferred_element_type=F32)


def _dot_nt(a, b):
    return lax.dot_general(a, b, NT_DIMS, preferred_element_type=F32)


def _softplus(z):
    neg_abs = lax.bitcast_convert_type(
        lax.bitcast_convert_type(z, jnp.int32) | jnp.int32(F32_SIGN_BIT), F32)
    return jnp.maximum(z, 0.0) + jnp.log(1.0 + jnp.exp(neg_abs))


def _layer_norm(y, g, b):
    mu = jnp.mean(y, axis=-1, keepdims=True)
    d = y - mu
    var = jnp.mean(d * d, axis=-1, keepdims=True)
    return d * lax.rsqrt(var + LN_EPS) * g + b


def _proj_kernel(x_ref, w32_ref, gup_ref, gbias_ref,
                 q_ref, k_ref, v_ref, gq_ref, gk_ref, gv_ref, gg_ref, la_ref, w_ref):
    @pl.when(pl.program_id(0) == 0)
    def _():
        w_ref[...] = w32_ref[...].astype(BF16)

    xb = x_ref[...].astype(BF16)
    col = 0
    for out_ref, scale in ((q_ref, SB_HEAD_DIM ** -0.5), (k_ref, None), (v_ref, None),
                           (gq_ref, GLA_KEY_DIM ** -0.5), (gk_ref, None), (gv_ref, None),
                           (gg_ref, None)):
        width = out_ref.shape[1]
        y = _dot_nt(xb, w_ref[col:col + width, :])
        if scale is not None:
            y = y * scale
        out_ref[...] = y.astype(out_ref.dtype)
        col += width
    ga = _dot_nt(xb, w_ref[col:col + GLA_GATE_RANK, :])
    logits = _dot(ga.astype(BF16), gup_ref[...].astype(BF16)) + gbias_ref[...]
    la_ref[...] = -_softplus(-logits) * (1.0 / GLA_TAU)


def _input_projection(x, w_in, layer, gate_up, gate_bias):
    w_t = jnp.swapaxes(w_in, 1, 2)
    in_width = MAIN_WIDTH + GLA_GATE_RANK
    tokens = x.shape[0]
    rows = PROJ_ROWS
    widths = (SB_WIDTH, SB_WIDTH, SB_WIDTH, GLA_KEYS, GLA_KEYS, GLA_WIDTH, GLA_WIDTH)
    const = lambda i: (0, 0)
    tile = lambda i: (i, 0)
    out_shape = [jax.ShapeDtypeStruct((tokens, w), BF16) for w in widths]
    out_shape.append(jax.ShapeDtypeStruct((tokens, GLA_KEYS), F32))
    out_specs = [pl.BlockSpec((rows, w), tile) for w in widths]
    out_specs.append(pl.BlockSpec((rows, GLA_KEYS), tile))
    return pl.pallas_call(
        _proj_kernel,
        out_shape=out_shape,
        grid=(tokens // rows,),
        in_specs=[
            pl.BlockSpec((rows, D_MODEL), tile),
            pl.BlockSpec((None, in_width, D_MODEL), lambda i: (layer, 0, 0)),
            pl.BlockSpec((GLA_GATE_RANK, GLA_KEYS), const),
            pl.BlockSpec((1, GLA_KEYS), const),
        ],
        out_specs=out_specs,
        scratch_shapes=[pltpu.VMEM((in_width, D_MODEL), BF16)],
        compiler_params=pltpu.CompilerParams(
            dimension_semantics=("arbitrary",), vmem_limit_bytes=VMEM_LIMIT_BYTES),
        name="input_projection",
    )(x, w_t, gate_up, gate_bias)


def _sb_kernel(q_ref, k_ref, v_ref, o_ref, qm_ref, acc_ref, carry_ref, z_ref, sp_ref):
    qi = pl.program_id(1)
    tq = SB_Q_ROWS
    n_sub = q_ref.shape[0] // tq
    win = SB_WINDOW
    n_pairs = SB_HEADS // 2
    lane = lax.broadcasted_iota(jnp.int32, (1, LANES), 1)
    even_lanes = lane < SB_HEAD_DIM
    row = lax.broadcasted_iota(jnp.int32, (win, win), 0)
    col = lax.broadcasted_iota(jnp.int32, (win, win), 1)
    neg_suffix_mat = jnp.where(row >= col, -1.0, 0.0).astype(BF16)
    key_col = lax.broadcasted_iota(jnp.int32, (2 * tq, win), 1)
    query_row = jnp.bitwise_and(lax.broadcasted_iota(jnp.int32, (2 * tq, win), 0), tq - 1)

    for s in range(n_sub):
        for p in range(n_pairs):
            qp = q_ref[s * tq:(s + 1) * tq, p * LANES:(p + 1) * LANES].astype(F32)
            qm_ref[s, p, 0:tq, :] = jnp.where(even_lanes, qp, 0.0).astype(BF16)
            qm_ref[s, p, tq:2 * tq, :] = jnp.where(even_lanes, 0.0, qp).astype(BF16)

    def visit_blocks(work, diagonal):
        valid = {}
        for s, key_start, limit in work:
            valid[s] = key_col < (limit + query_row if diagonal else limit)
            for p in range(n_pairs):
                kp = k_ref[pl.ds(key_start, win), p * LANES:(p + 1) * LANES]
                z = _dot_nt(qm_ref[s, p], kp)
                z_ref[s, p] = z
                sp_ref[s, p] = jnp.where(valid[s], _softplus(z), 0.0).astype(BF16)
        for s, key_start, limit in work:
            for p in range(n_pairs):
                pcols = slice(p * LANES, (p + 1) * LANES)
                incl = _dot(sp_ref[s, p], neg_suffix_mat)
                log_w = z_ref[s, p] + incl
                if not diagonal:
                    carry = carry_ref[s, p]
                    log_w = log_w + jnp.concatenate([carry] * (win // LANES), axis=1)
                w = jnp.exp(jnp.where(valid[s], log_w, SB_MASKED_LOG_WEIGHT)).astype(BF16)
                d = _dot(w, v_ref[pl.ds(key_start, win), pcols])
                pv = jnp.where(even_lanes, d[0:tq], d[tq:2 * tq])
                block_total = incl[:, 0:1]
                if diagonal:
                    acc_ref[s, :, pcols] = pv
                    carry_ref[s, p] = jnp.broadcast_to(block_total, (2 * tq, LANES))
                else:
                    acc_ref[s, :, pcols] += pv
                    carry_ref[s, p] = carry + block_total

    def any_row_alive(s):
        m = carry_ref[s, 0]
        for p in range(1, n_pairs):
            m = jnp.maximum(m, carry_ref[s, p])
        return (jnp.max(m) > SB_DEAD_LOG_WEIGHT).astype(jnp.int32)

    q0 = [(qi * n_sub + s) * tq for s in range(n_sub)]
    start = [jnp.maximum(q0[s] + tq - win, 0) for s in range(n_sub)]
    visit_blocks([(s, pl.multiple_of(start[s], tq), q0[s] - start[s]) for s in range(n_sub)], diagonal=True)

    def cond(state):
        visited_from, alive = state
        return jnp.logical_and(visited_from > 0, alive > 0)

    alive = [any_row_alive(s) for s in range(n_sub)]
    for s in range(n_sub):
        def body(state, s=s):
            visited_from, _ = state
            block_start = jnp.maximum(visited_from - win, 0)
            visit_blocks([(s, pl.multiple_of(block_start, tq), visited_from - block_start)], diagonal=False)
            return block_start, any_row_alive(s)

        lax.while_loop(cond, body, (start[s], alive[s]))
        o_ref[s * tq:(s + 1) * tq, :] = acc_ref[s].astype(o_ref.dtype)


def _stick_breaking(q, k, v, batch, seq):
    tq = SB_Q_ROWS
    n_sub = SB_TILES_PER_STEP
    assert seq >= SB_WINDOW and SB_WINDOW % tq == 0
    n_q = seq // (tq * n_sub)
    n_pairs = SB_HEADS // 2
    return pl.pallas_call(
        _sb_kernel,
        out_shape=jax.ShapeDtypeStruct(q.shape, BF16),
        grid=(batch, n_q),
        in_specs=[
            pl.BlockSpec((n_sub * tq, SB_WIDTH), lambda b, i: (b * n_q + i, 0)),
            pl.BlockSpec((seq, SB_WIDTH), lambda b, i: (b, 0)),
            pl.BlockSpec((seq, SB_WIDTH), lambda b, i: (b, 0)),
        ],
        out_specs=pl.BlockSpec((n_sub * tq, SB_WIDTH), lambda b, i: (b * n_q + i, 0)),
        scratch_shapes=[
            pltpu.VMEM((n_sub, n_pairs, 2 * tq, LANES), BF16),
            pltpu.VMEM((n_sub, tq, SB_WIDTH), F32),
            pltpu.VMEM((n_sub, n_pairs, 2 * tq, LANES), F32),
            pltpu.VMEM((n_sub, n_pairs, 2 * tq, SB_WINDOW), F32),
            pltpu.VMEM((n_sub, n_pairs, 2 * tq, SB_WINDOW), BF16),
        ],
        compiler_params=pltpu.CompilerParams(
            dimension_semantics=("parallel", "parallel"), vmem_limit_bytes=VMEM_LIMIT_BYTES),
        name="stick_breaking_attention",
    )(q, k, v)


def _gla_kernel(q_ref, k_ref, v_ref, gg_ref, la_ref, ng_ref, o_ref, st_ref):
    @pl.when(pl.program_id(0) == 0)
    def _():
        st_ref[...] = jnp.zeros_like(st_ref)

    ch = GLA_CHUNK
    n_batch, n_rows = q_ref.shape[0], q_ref.shape[1]
    n_ch = n_rows // ch
    n_pairs = GLA_HEADS // 2
    shift = ch.bit_length() - 1
    r = lax.broadcasted_iota(jnp.int32, (n_rows, n_rows), 0)
    c = lax.broadcasted_iota(jnp.int32, (n_rows, n_rows), 1)
    same_chunk = jnp.right_shift(r, shift) == jnp.right_shift(c, shift)
    gate_mats = jnp.concatenate([
        jnp.where(jnp.logical_and(same_chunk, c <= r), 1.0, 0.0),
        jnp.where(jnp.logical_and(same_chunk, jnp.bitwise_and(c, ch - 1) < ch // 2), 1.0, 0.0),
        jnp.where(same_chunk, 1.0, 0.0)], axis=0).astype(BF16)
    r2 = lax.broadcasted_iota(jnp.int32, (2 * ch, 2 * ch), 0)
    c2 = lax.broadcasted_iota(jnp.int32, (2 * ch, 2 * ch), 1)
    same_head_causal = jnp.logical_and(jnp.bitwise_and(r2, ch) == jnp.bitwise_and(c2, ch),
                                       jnp.bitwise_and(c2, ch - 1) <= jnp.bitwise_and(r2, ch - 1))
    lane = lax.broadcasted_iota(jnp.int32, (1, LANES), 1)
    even_lanes = lane < GLA_KEY_DIM
    norm_g = ng_ref[...]

    def stack_masked(t):
        return jnp.concatenate([jnp.where(even_lanes, t, 0.0), jnp.where(even_lanes, 0.0, t)],
                               axis=0).astype(BF16)

    def stack_heads(t):
        return jnp.concatenate([t[:, :GLA_VAL_DIM], t[:, GLA_VAL_DIM:]], axis=0)

    bodies = [(bi, ci, p) for bi in range(n_batch) for ci in range(n_ch) for p in range(n_pairs)]
    rows_of = lambda ci: slice(ci * ch, (ci + 1) * ch)
    pair_of = lambda p: slice(p * LANES, (p + 1) * LANES)
    vcols_of = lambda p: slice(2 * p * GLA_VAL_DIM, (2 * p + 2) * GLA_VAL_DIM)

    gated = []
    for bi in range(n_batch):
        g_hi, g_lo = _split_bf16(la_ref[bi])
        b3 = _dot(gate_mats, g_hi) + _dot(gate_mats, g_lo)
        b, b_mid, b_last = b3[0:n_rows], b3[n_rows:2 * n_rows], b3[2 * n_rows:]
        q = q_ref[bi].astype(F32)
        k = k_ref[bi].astype(F32)
        k_in = k * jnp.exp(b_mid - b)
        gated.append(dict(q_in=q * jnp.exp(b - b_mid), k_in=k_in, k_in_b=k_in.astype(BF16),
                          b_mid=b_mid, b_last=b_last))

    v_st, scores = {}, {}
    for bi, ci, p in bodies:
        rows, pair = rows_of(ci), pair_of(p)
        v_st[bi, ci, p] = stack_heads(v_ref[bi, rows, vcols_of(p)])
        k_pair = gated[bi]["k_in_b"][rows, pair]
        s = _dot_nt(stack_masked(gated[bi]["q_in"][rows, pair]), jnp.concatenate([k_pair, k_pair], axis=0))
        scores[bi, ci, p] = jnp.where(same_head_causal, s, 0.0).astype(BF16)

    o_intra, upd = {}, {}
    for body in bodies:
        bi, ci, p = body
        o_intra[body] = _dot(scores[body], v_st[body])
        row, pair = slice(ci * ch, ci * ch + 1), pair_of(p)
        k_dec = gated[bi]["k_in"][rows_of(ci), pair] * jnp.exp(
            gated[bi]["b_last"][row, pair] - gated[bi]["b_mid"][row, pair])
        upd[body] = _dot(v_st[body].astype(F32).T.astype(BF16), stack_masked(k_dec))

    st_in = {}
    for bi in range(n_batch):
        for p in range(n_pairs):
            st = st_ref[bi, p]
            for ci in range(n_ch):
                st_in[bi, ci, p] = st
                decay = jnp.exp(gated[bi]["b_last"][ci * ch:ci * ch + 1, pair_of(p)])
                st = st * decay + upd[bi, ci, p]
            st_ref[bi, p] = st

    for body in bodies:
        bi, ci, p = body
        rows, pair = rows_of(ci), pair_of(p)
        q_b = gated[bi]["q_in"][rows, pair] * jnp.exp(gated[bi]["b_mid"][ci * ch:ci * ch + 1, pair])
        o = o_intra[body] + _dot_nt(stack_masked(q_b), st_in[body].astype(BF16))
        on = o * lax.rsqrt(jnp.mean(o * o, axis=-1, keepdims=True) + RMS_EPS) * norm_g
        gate = stack_heads(gg_ref[bi, rows, vcols_of(p)].astype(F32))
        res = (on * (gate / (1.0 + jnp.exp(-gate)))).astype(o_ref.dtype)
        o_ref[bi, rows, vcols_of(p)] = jnp.concatenate([res[:ch], res[ch:]], axis=1)


def _gla(gq, gk, gv, gg, log_a, norm_g, batch, seq):
    rows = GLA_ROWS
    shaped = lambda t: t.reshape(batch, seq, t.shape[-1])
    tile = lambda i: (0, i, 0)
    out = pl.pallas_call(
        _gla_kernel,
        out_shape=jax.ShapeDtypeStruct((batch, seq, GLA_WIDTH), BF16),
        grid=(seq // rows,),
        in_specs=[
            pl.BlockSpec((batch, rows, GLA_KEYS), tile),
            pl.BlockSpec((batch, rows, GLA_KEYS), tile),
            pl.BlockSpec((batch, rows, GLA_WIDTH), tile),
            pl.BlockSpec((batch, rows, GLA_WIDTH), tile),
            pl.BlockSpec((batch, rows, GLA_KEYS), tile),
            pl.BlockSpec((1, GLA_VAL_DIM), lambda i: (0, 0)),
        ],
        out_specs=pl.BlockSpec((batch, rows, GLA_WIDTH), tile),
        scratch_shapes=[pltpu.VMEM((batch, GLA_HEADS // 2, GLA_VAL_DIM, LANES), F32)],
        compiler_params=pltpu.CompilerParams(
            dimension_semantics=("arbitrary",), vmem_limit_bytes=VMEM_LIMIT_BYTES),
        name="gated_linear_attention",
    )(shaped(gq), shaped(gk), shaped(gv), shaped(gg), shaped(log_a), norm_g)
    return out.reshape(batch * seq, GLA_WIDTH)


def _mix_ffn_kernel(alpha, tiles_per_seq, sb_ref, gla_ref, x_ref, wo_ref, g1_ref, b1_ref,
                    wup_ref, cw_ref, cb_ref, wd_ref, g2_ref, b2_ref, o_ref,
                    h_ref, ua_ref, uc_ref, act_ref):
    rows = x_ref.shape[0]
    n_chunks = D_FF // FFN_CHUNK

    @pl.when(pl.program_id(0) % tiles_per_seq == 0)
    def _():
        ua_ref[:, 0:CONV_TAIL, :] = jnp.zeros((n_chunks, CONV_TAIL, FFN_CHUNK), F32)
        uc_ref[:, 0:CONV_TAIL, :] = jnp.zeros((n_chunks, CONV_TAIL, FFN_CHUNK), F32)

    mix = _dot(sb_ref[...], wo_ref[0:SB_WIDTH, :]) + _dot(gla_ref[...], wo_ref[SB_WIDTH:, :])
    h = _layer_norm(alpha * x_ref[...] + mix, g1_ref[...], b1_ref[...])
    h_ref[...] = h
    hb = h.astype(BF16)

    def conv(u_ref, slot, c0, scale):
        w = cw_ref[:, c0:c0 + FFN_CHUNK] * scale
        return (w[2:3, :] * u_ref[slot, CONV_TAIL:CONV_TAIL + rows, :]
                + w[1:2, :] * u_ref[slot, CONV_TAIL - 1:CONV_TAIL - 1 + rows, :]
                + w[0:1, :] * u_ref[slot, CONV_TAIL - 2:CONV_TAIL - 2 + rows, :]
                + cb_ref[:, c0:c0 + FFN_CHUNK] * scale)

    for ci in range(n_chunks):
        a0 = ci * FFN_CHUNK
        c0 = D_FF + ci * FFN_CHUNK
        ua_ref[ci, CONV_TAIL:, :] = _dot(hb, wup_ref[:, a0:a0 + FFN_CHUNK])
        uc_ref[ci, CONV_TAIL:, :] = _dot(hb, wup_ref[:, c0:c0 + FFN_CHUNK])
        half_a = conv(ua_ref, ci, a0, 0.5)
        c = conv(uc_ref, ci, c0, 1.0)
        act_ref[:, a0:a0 + FFN_CHUNK] = (
            half_a * (1.0 + lax.erf(half_a * (2.0 ** 0.5))) * c).astype(BF16)
        for u_ref in (ua_ref, uc_ref):
            u_ref[ci, 0:CONV_TAIL, :] = u_ref[ci, rows:rows + CONV_TAIL, :]

    f = _dot(act_ref[...], wd_ref[...])
    o_ref[...] = _layer_norm(alpha * h_ref[...] + f, g2_ref[...], b2_ref[...])


def _mix_ffn(sb_o, gla_o, x, w_out, g1, b1, w_up, conv_w, conv_b, w_down, g2, b2, alpha, seq):
    tokens = x.shape[0]
    rows = FFN_ROWS
    n_chunks = D_FF // FFN_CHUNK
    assert n_chunks * FFN_CHUNK == D_FF and seq % rows == 0
    tile = lambda i: (i, 0)
    const = lambda i: (0, 0)
    whole = lambda t: pl.BlockSpec(t.shape, const)
    return pl.pallas_call(
        functools.partial(_mix_ffn_kernel, alpha, seq // rows),
        out_shape=jax.ShapeDtypeStruct((tokens, D_MODEL), F32),
        grid=(tokens // rows,),
        in_specs=[
            pl.BlockSpec((rows, SB_WIDTH), tile),
            pl.BlockSpec((rows, GLA_WIDTH), tile),
            pl.BlockSpec((rows, D_MODEL), tile),
            whole(w_out), whole(g1), whole(b1),
            whole(w_up), whole(conv_w), whole(conv_b), whole(w_down), whole(g2), whole(b2),
        ],
        out_specs=pl.BlockSpec((rows, D_MODEL), tile),
        scratch_shapes=[
            pltpu.VMEM((rows, D_MODEL), F32),
            pltpu.VMEM((n_chunks, CONV_TAIL + rows, FFN_CHUNK), F32),
            pltpu.VMEM((n_chunks, CONV_TAIL + rows, FFN_CHUNK), F32),
            pltpu.VMEM((rows, D_FF), BF16),
        ],
        compiler_params=pltpu.CompilerParams(
            dimension_semantics=("arbitrary",), vmem_limit_bytes=VMEM_LIMIT_BYTES),
        name="mix_ffn",
    )(sb_o, gla_o, x, w_out, g1, b1, w_up, conv_w, conv_b, w_down, g2, b2)


def _layer(x, w_in, layer, gate_up, gate_bias, gla_norm_g, w_out, ln1_g, ln1_b,
           w_up, conv_w, conv_b, w_down, ln2_g, ln2_b, alpha, batch, seq):
    row = lambda t: t.reshape(1, -1)
    q, k, v, gq, gk, gv, gg, log_a = _input_projection(
        x, w_in, layer, gate_up, row(gate_bias))
    sb_o = _stick_breaking(q, k, v, batch, seq)
    gla_o = _gla(gq, gk, gv, gg, log_a, row(gla_norm_g), batch, seq)
    return _mix_ffn(sb_o, gla_o, x, w_out.astype(BF16), row(ln1_g), row(ln1_b),
                    w_up.astype(BF16), conv_w, row(conv_b), w_down.astype(BF16),
                    row(ln2_g), row(ln2_b), alpha, seq)


def kernel(x, w_in, gate_up, gate_bias, gla_norm_g, w_out, ln1_g, ln1_b, w_up, conv_w, conv_b,
           w_down, ln2_g, ln2_b):
    batch, seq, d_model = x.shape
    depth = w_in.shape[0]
    alpha = (2.0 * depth) ** 0.25
    y = x.reshape(batch * seq, d_model)
    for l in range(depth):
        y = _layer(y, w_in, l, gate_up[l], gate_bias[l], gla_norm_g[l], w_out[l], ln1_g[l], ln1_b[l],
                   w_up[l], conv_w[l], conv_b[l], w_down[l], ln2_g[l], ln2_b[l], alpha, batch, seq)
    return y.reshape(batch, seq, d_model)
```

```python
import functools

import jax
import jax.numpy as jnp
from jax import lax
from jax.experimental import pallas as pl
from jax.experimental.pallas import tpu as pltpu

F32 = jnp.float32
BF16 = jnp.bfloat16

D_MODEL = 1024
SB_WIDTH = 512
SB_HEADS = 8
SB_HEAD_DIM = 64
GLA_WIDTH = 512
GLA_HEADS = 4
GLA_KEY_DIM = 64
GLA_VAL_DIM = 128
GLA_KEYS = GLA_HEADS * GLA_KEY_DIM
GLA_GATE_RANK = 16
GLA_TAU = 16.0
GLA_CHUNK = 64
D_FF = 2816
CONV_WIDTH = 3
LN_EPS = 1e-5
RMS_EPS = 1e-6
MAIN_WIDTH = 3 * SB_WIDTH + 2 * GLA_KEYS + 2 * GLA_WIDTH

LANES = 128
SUBLANES = 8
MXU_WIDTH = 256
VMEM_LIMIT_BYTES = 56 * 1024 * 1024

PROJ_ROWS = 1024
SB_Q_ROWS = 64
SB_WINDOW = MXU_WIDTH
SB_TILES_PER_STEP = 16
GLA_ROWS = 256
FFN_ROWS = 512
FFN_CHUNK = MXU_WIDTH
CONV_TAIL = SUBLANES

SB_DEAD_LOG_WEIGHT = -90.0
SB_MASKED_SCORE = -1e30
F32_SIGN_BIT = -2 ** 31

NT_DIMS = (((1,), (1,)), ((), ()))


def _split_bf16(x):
    hi = x.astype(BF16)
    lo = (x - hi.astype(F32)).astype(BF16)
    return hi, lo


def _dot(a, b):
    return jnp.dot(a, b, preferred_element_type=F32)


def _dot_nt(a, b):
    return lax.dot_general(a, b, NT_DIMS, preferred_element_type=F32)


def _softplus(z):
    neg_abs = lax.bitcast_convert_type(
        lax.bitcast_convert_type(z, jnp.int32) | jnp.int32(F32_SIGN_BIT), F32)
    return jnp.maximum(z, 0.0) + jnp.log(1.0 + jnp.exp(neg_abs))


def _layer_norm(y, g, b):
    mu = jnp.mean(y, axis=-1, keepdims=True)
    d = y - mu
    var = jnp.mean(d * d, axis=-1, keepdims=True)
    return d * lax.rsqrt(var + LN_EPS) * g + b


def _proj_kernel(x_ref, w32_ref, gup_ref, gbias_ref,
                 q_ref, k_ref, v_ref, gq_ref, gk_ref, gv_ref, gg_ref, la_ref, w_ref):
    @pl.when(pl.program_id(0) == 0)
    def _():
        w_ref[...] = w32_ref[...].astype(BF16)

    xb = x_ref[...].astype(BF16)
    col = 0
    for out_ref, scale in ((q_ref, SB_HEAD_DIM ** -0.5), (k_ref, None), (v_ref, None),
                           (gq_ref, GLA_KEY_DIM ** -0.5), (gk_ref, None), (gv_ref, None),
                           (gg_ref, None)):
        width = out_ref.shape[1]
        y = _dot_nt(xb, w_ref[col:col + width, :])
        if scale is not None:
            y = y * scale
        out_ref[...] = y.astype(out_ref.dtype)
        col += width
    ga = _dot_nt(xb, w_ref[col:col + GLA_GATE_RANK, :])
    logits = _dot(ga.astype(BF16), gup_ref[...].astype(BF16)) + gbias_ref[...]
    la_ref[...] = -_softplus(-logits) * (1.0 / GLA_TAU)


def _input_projection(x, w_in, layer, gate_up, gate_bias):
    w_t = jnp.swapaxes(w_in, 1, 2)
    in_width = MAIN_WIDTH + GLA_GATE_RANK
    tokens = x.shape[0]
    rows = PROJ_ROWS
    widths = (SB_WIDTH, SB_WIDTH, SB_WIDTH, GLA_KEYS, GLA_KEYS, GLA_WIDTH, GLA_WIDTH)
    const = lambda i: (0, 0)
    tile = lambda i: (i, 0)
    out_shape = [jax.ShapeDtypeStruct((tokens, w), BF16) for w in widths]
    out_shape.append(jax.ShapeDtypeStruct((tokens, GLA_KEYS), F32))
    out_specs = [pl.BlockSpec((rows, w), tile) for w in widths]
    out_specs.append(pl.BlockSpec((rows, GLA_KEYS), tile))
    return pl.pallas_call(
        _proj_kernel,
        out_shape=out_shape,
        grid=(tokens // rows,),
        in_specs=[
            pl.BlockSpec((rows, D_MODEL), tile),
            pl.BlockSpec((None, in_width, D_MODEL), lambda i: (layer, 0, 0)),
            pl.BlockSpec((GLA_GATE_RANK, GLA_KEYS), const),
            pl.BlockSpec((1, GLA_KEYS), const),
        ],
        out_specs=out_specs,
        scratch_shapes=[pltpu.VMEM((in_width, D_MODEL), BF16)],
        compiler_params=pltpu.CompilerParams(
            dimension_semantics=("arbitrary",), vmem_limit_bytes=VMEM_LIMIT_BYTES),
        name="input_projection",
    )(x, w_t, gate_up, gate_bias)


def _sb_kernel(q_ref, k_ref, v_ref, o_ref, qm_ref, acc_ref, carry_ref, z_ref, sp_ref):
    qi = pl.program_id(1)
    tq = SB_Q_ROWS
    n_sub = q_ref.shape[0] // tq
    win = SB_WINDOW
    n_pairs = SB_HEADS // 2
    lane = lax.broadcasted_iota(jnp.int32, (1, LANES), 1)
    even_lanes = lane < SB_HEAD_DIM
    row = lax.broadcasted_iota(jnp.int32, (win, win), 0)
    col = lax.broadcasted_iota(jnp.int32, (win, win), 1)
    neg_suffix_mat = jnp.where(row >= col, -1.0, 0.0).astype(BF16)
    key_col = lax.broadcasted_iota(jnp.int32, (2 * tq, win), 1)
    query_row = jnp.bitwise_and(lax.broadcasted_iota(jnp.int32, (2 * tq, win), 0), tq - 1)

    for s in range(n_sub):
        for p in range(n_pairs):
            qp = q_ref[s * tq:(s + 1) * tq, p * LANES:(p + 1) * LANES].astype(F32)
            qm_ref[s, p, 0:tq, :] = jnp.where(even_lanes, qp, 0.0).astype(BF16)
            qm_ref[s, p, tq:2 * tq, :] = jnp.where(even_lanes, 0.0, qp).astype(BF16)

    def visit_blocks(work, diagonal):
        for s, key_start, limit in work:
            valid = key_col < (limit + query_row if diagonal else limit)
            for p in range(n_pairs):
                kp = k_ref[pl.ds(key_start, win), p * LANES:(p + 1) * LANES]
                z = jnp.where(valid, _dot_nt(qm_ref[s, p], kp), SB_MASKED_SCORE)
                z_ref[s, p] = z
                sp_ref[s, p] = _softplus(z).astype(BF16)
        for s, key_start, limit in work:
            for p in range(n_pairs):
                pcols = slice(p * LANES, (p + 1) * LANES)
                incl = _dot(sp_ref[s, p], neg_suffix_mat)
                log_w = z_ref[s, p] + incl
                if not diagonal:
                    carry = carry_ref[s, p]
                    log_w = log_w + jnp.concatenate([carry] * (win // LANES), axis=1)
                w = jnp.exp(log_w).astype(BF16)
                d = _dot(w, v_ref[pl.ds(key_start, win), pcols])
                pv = jnp.where(even_lanes, d[0:tq], d[tq:2 * tq])
                block_total = incl[:, 0:1]
                if diagonal:
                    acc_ref[s, :, pcols] = pv
                    carry_ref[s, p] = jnp.broadcast_to(block_total, (2 * tq, LANES))
                else:
                    acc_ref[s, :, pcols] += pv
                    carry_ref[s, p] = carry + block_total

    def any_row_alive(s):
        m = carry_ref[s, 0]
        for p in range(1, n_pairs):
            m = jnp.maximum(m, carry_ref[s, p])
        return (jnp.max(m) > SB_DEAD_LOG_WEIGHT).astype(jnp.int32)

    q0 = [(qi * n_sub + s) * tq for s in range(n_sub)]
    start = [jnp.maximum(q0[s] + tq - win, 0) for s in range(n_sub)]
    visit_blocks([(s, pl.multiple_of(start[s], tq), q0[s] - start[s]) for s in range(n_sub)], diagonal=True)

    def cond(state):
        visited_from, alive = state
        return jnp.logical_and(visited_from > 0, alive > 0)

    alive = [any_row_alive(s) for s in range(n_sub)]
    for s in range(n_sub):
        def body(state, s=s):
            visited_from, _ = state
            block_start = jnp.maximum(visited_from - win, 0)
            visit_blocks([(s, pl.multiple_of(block_start, tq), visited_from - block_start)], diagonal=False)
            return block_start, any_row_alive(s)

        lax.while_loop(cond, body, (start[s], alive[s]))
        o_ref[s * tq:(s + 1) * tq, :] = acc_ref[s].astype(o_ref.dtype)


def _stick_breaking(q, k, v, batch, seq):
    tq = SB_Q_ROWS
    n_sub = SB_TILES_PER_STEP
    assert seq >= SB_WINDOW and SB_WINDOW % tq == 0
    n_q = seq // (tq * n_sub)
    n_pairs = SB_HEADS // 2
    return pl.pallas_call(
        _sb_kernel,
        out_shape=jax.ShapeDtypeStruct(q.shape, BF16),
        grid=(batch, n_q),
        in_specs=[
            pl.BlockSpec((n_sub * tq, SB_WIDTH), lambda b, i: (b * n_q + i, 0)),
            pl.BlockSpec((seq, SB_WIDTH), lambda b, i: (b, 0)),
            pl.BlockSpec((seq, SB_WIDTH), lambda b, i: (b, 0)),
        ],
        out_specs=pl.BlockSpec((n_sub * tq, SB_WIDTH), lambda b, i: (b * n_q + i, 0)),
        scratch_shapes=[
            pltpu.VMEM((n_sub, n_pairs, 2 * tq, LANES), BF16),
            pltpu.VMEM((n_sub, tq, SB_WIDTH), F32),
            pltpu.VMEM((n_sub, n_pairs, 2 * tq, LANES), F32),
            pltpu.VMEM((n_sub, n_pairs, 2 * tq, SB_WINDOW), F32),
            pltpu.VMEM((n_sub, n_pairs, 2 * tq, SB_WINDOW), BF16),
        ],
        compiler_params=pltpu.CompilerParams(
            dimension_semantics=("parallel", "parallel"), vmem_limit_bytes=VMEM_LIMIT_BYTES),
        name="stick_breaking_attention",
    )(q, k, v)


def _gla_kernel(q_ref, k_ref, v_ref, gg_ref, la_ref, ng_ref, o_ref, st_ref):
    @pl.when(pl.program_id(0) == 0)
    def _():
        st_ref[...] = jnp.zeros_like(st_ref)

    ch = GLA_CHUNK
    n_batch, n_rows = q_ref.shape[0], q_ref.shape[1]
    n_ch = n_rows // ch
    n_pairs = GLA_HEADS // 2
    shift = ch.bit_length() - 1
    r = lax.broadcasted_iota(jnp.int32, (n_rows, n_rows), 0)
    c = lax.broadcasted_iota(jnp.int32, (n_rows, n_rows), 1)
    same_chunk = jnp.right_shift(r, shift) == jnp.right_shift(c, shift)
    gate_mats = jnp.concatenate([
        jnp.where(jnp.logical_and(same_chunk, c <= r), 1.0, 0.0),
        jnp.where(jnp.logical_and(same_chunk, jnp.bitwise_and(c, ch - 1) < ch // 2), 1.0, 0.0),
        jnp.where(same_chunk, 1.0, 0.0)], axis=0).astype(BF16)
    r2 = lax.broadcasted_iota(jnp.int32, (2 * ch, 2 * ch), 0)
    c2 = lax.broadcasted_iota(jnp.int32, (2 * ch, 2 * ch), 1)
    same_head_causal = jnp.logical_and(jnp.bitwise_and(r2, ch) == jnp.bitwise_and(c2, ch),
                                       jnp.bitwise_and(c2, ch - 1) <= jnp.bitwise_and(r2, ch - 1))
    lane = lax.broadcasted_iota(jnp.int32, (1, LANES), 1)
    even_lanes = lane < GLA_KEY_DIM
    norm_g = ng_ref[...]

    def stack_masked(t):
        return jnp.concatenate([jnp.where(even_lanes, t, 0.0), jnp.where(even_lanes, 0.0, t)],
                               axis=0).astype(BF16)

    def stack_heads(t):
        return jnp.concatenate([t[:, :GLA_VAL_DIM], t[:, GLA_VAL_DIM:]], axis=0)

    bodies = [(bi, ci, p) for bi in range(n_batch) for ci in range(n_ch) for p in range(n_pairs)]
    rows_of = lambda ci: slice(ci * ch, (ci + 1) * ch)
    pair_of = lambda p: slice(p * LANES, (p + 1) * LANES)
    vcols_of = lambda p: slice(2 * p * GLA_VAL_DIM, (2 * p + 2) * GLA_VAL_DIM)

    gated = []
    for bi in range(n_batch):
        g_hi, g_lo = _split_bf16(la_ref[bi])
        b3 = _dot(gate_mats, g_hi) + _dot(gate_mats, g_lo)
        b, b_mid, b_last = b3[0:n_rows], b3[n_rows:2 * n_rows], b3[2 * n_rows:]
        q = q_ref[bi].astype(F32)
        k = k_ref[bi].astype(F32)
        k_in = k * jnp.exp(b_mid - b)
        gated.append(dict(q_in=q * jnp.exp(b - b_mid), k_in=k_in, k_in_b=k_in.astype(BF16),
                          b_mid=b_mid, b_last=b_last))

    v_st, scores = {}, {}
    for bi, ci, p in bodies:
        rows, pair = rows_of(ci), pair_of(p)
        v_st[bi, ci, p] = stack_heads(v_ref[bi, rows, vcols_of(p)])
        k_pair = gated[bi]["k_in_b"][rows, pair]
        s = _dot_nt(stack_masked(gated[bi]["q_in"][rows, pair]), jnp.concatenate([k_pair, k_pair], axis=0))
        scores[bi, ci, p] = jnp.where(same_head_causal, s, 0.0).astype(BF16)

    o_intra, upd = {}, {}
    for body in bodies:
        bi, ci, p = body
        o_intra[body] = _dot(scores[body], v_st[body])
        row, pair = slice(ci * ch, ci * ch + 1), pair_of(p)
        k_dec = gated[bi]["k_in"][rows_of(ci), pair] * jnp.exp(
            gated[bi]["b_last"][row, pair] - gated[bi]["b_mid"][row, pair])
        upd[body] = _dot(v_st[body].astype(F32).T.astype(BF16), stack_masked(k_dec))

    st_in = {}
    for bi in range(n_batch):
        for p in range(n_pairs):
            st = st_ref[bi, p]
            for ci in range(n_ch):
                st_in[bi, ci, p] = st
                decay = jnp.exp(gated[bi]["b_last"][ci * ch:ci * ch + 1, pair_of(p)])
                st = st * decay + upd[bi, ci, p]
            st_ref[bi, p] = st

    for body in bodies:
        bi, ci, p = body
        rows, pair = rows_of(ci), pair_of(p)
        q_b = gated[bi]["q_in"][rows, pair] * jnp.exp(gated[bi]["b_mid"][ci * ch:ci * ch + 1, pair])
        o = o_intra[body] + _dot_nt(stack_masked(q_b), st_in[body].astype(BF16))
        on = o * lax.rsqrt(jnp.mean(o * o, axis=-1, keepdims=True) + RMS_EPS) * norm_g
        gate = stack_heads(gg_ref[bi, rows, vcols_of(p)].astype(F32))
        res = (on * (gate / (1.0 + jnp.exp(-gate)))).astype(o_ref.dtype)
        o_ref[bi, rows, vcols_of(p)] = jnp.concatenate([res[:ch], res[ch:]], axis=1)


def _gla(gq, gk, gv, gg, log_a, norm_g, batch, seq):
    rows = GLA_ROWS
    shaped = lambda t: t.reshape(batch, seq, t.shape[-1])
    tile = lambda i: (0, i, 0)
    out = pl.pallas_call(
        _gla_kernel,
        out_shape=jax.ShapeDtypeStruct((batch, seq, GLA_WIDTH), BF16),
        grid=(seq // rows,),
        in_specs=[
            pl.BlockSpec((batch, rows, GLA_KEYS), tile),
            pl.BlockSpec((batch, rows, GLA_KEYS), tile),
            pl.BlockSpec((batch, rows, GLA_WIDTH), tile),
            pl.BlockSpec((batch, rows, GLA_WIDTH), tile),
            pl.BlockSpec((batch, rows, GLA_KEYS), tile),
            pl.BlockSpec((1, GLA_VAL_DIM), lambda i: (0, 0)),
        ],
        out_specs=pl.BlockSpec((batch, rows, GLA_WIDTH), tile),
        scratch_shapes=[pltpu.VMEM((batch, GLA_HEADS // 2, GLA_VAL_DIM, LANES), F32)],
        compiler_params=pltpu.CompilerParams(
            dimension_semantics=("arbitrary",), vmem_limit_bytes=VMEM_LIMIT_BYTES),
        name="gated_linear_attention",
    )(shaped(gq), shaped(gk), shaped(gv), shaped(gg), shaped(log_a), norm_g)
    return out.reshape(batch * seq, GLA_WIDTH)


def _mix_ffn_kernel(alpha, tiles_per_seq, sb_ref, gla_ref, x_ref, wo_ref, g1_ref, b1_ref,
                    wup_ref, cw_ref, cb_ref, wd_ref, g2_ref, b2_ref, o_ref,
                    h_ref, ua_ref, uc_ref, act_ref):
    rows = x_ref.shape[0]
    n_chunks = D_FF // FFN_CHUNK

    @pl.when(pl.program_id(0) % tiles_per_seq == 0)
    def _():
        ua_ref[:, 0:CONV_TAIL, :] = jnp.zeros((n_chunks, CONV_TAIL, FFN_CHUNK), F32)
        uc_ref[:, 0:CONV_TAIL, :] = jnp.zeros((n_chunks, CONV_TAIL, FFN_CHUNK), F32)

    mix = _dot(sb_ref[...], wo_ref[0:SB_WIDTH, :]) + _dot(gla_ref[...], wo_ref[SB_WIDTH:, :])
    h = _layer_norm(alpha * x_ref[...] + mix, g1_ref[...], b1_ref[...])
    h_ref[...] = h
    hb = h.astype(BF16)

    def conv(u_ref, slot, c0, scale):
        w = cw_ref[:, c0:c0 + FFN_CHUNK] * scale
        return (w[2:3, :] * u_ref[slot, CONV_TAIL:CONV_TAIL + rows, :]
                + w[1:2, :] * u_ref[slot, CONV_TAIL - 1:CONV_TAIL - 1 + rows, :]
                + w[0:1, :] * u_ref[slot, CONV_TAIL - 2:CONV_TAIL - 2 + rows, :]
                + cb_ref[:, c0:c0 + FFN_CHUNK] * scale)

    for ci in range(n_chunks):
        a0 = ci * FFN_CHUNK
        c0 = D_FF + ci * FFN_CHUNK
        ua_ref[ci, CONV_TAIL:, :] = _dot(hb, wup_ref[:, a0:a0 + FFN_CHUNK])
        uc_ref[ci, CONV_TAIL:, :] = _dot(hb, wup_ref[:, c0:c0 + FFN_CHUNK])
        half_a = conv(ua_ref, ci, a0, 0.5)
        c = conv(uc_ref, ci, c0, 1.0)
        act_ref[:, a0:a0 + FFN_CHUNK] = (
            half_a * (1.0 + lax.erf(half_a * (2.0 ** 0.5))) * c).astype(BF16)
        for u_ref in (ua_ref, uc_ref):
            u_ref[ci, 0:CONV_TAIL, :] = u_ref[ci, rows:rows + CONV_TAIL, :]

    f = _dot(act_ref[...], wd_ref[...])
    o_ref[...] = _layer_norm(alpha * h_ref[...] + f, g2_ref[...], b2_ref[...])


def _mix_ffn(sb_o, gla_o, x, w_out, g1, b1, w_up, conv_w, conv_b, w_down, g2, b2, alpha, seq):
    tokens = x.shape[0]
    rows = FFN_ROWS
    n_chunks = D_FF // FFN_CHUNK
    assert n_chunks * FFN_CHUNK == D_FF and seq % rows == 0
    tile = lambda i: (i, 0)
    const = lambda i: (0, 0)
    whole = lambda t: pl.BlockSpec(t.shape, const)
    return pl.pallas_call(
        functools.partial(_mix_ffn_kernel, alpha, seq // rows),
        out_shape=jax.ShapeDtypeStruct((tokens, D_MODEL), F32),
        grid=(tokens // rows,),
        in_specs=[
            pl.BlockSpec((rows, SB_WIDTH), tile),
            pl.BlockSpec((rows, GLA_WIDTH), tile),
            pl.BlockSpec((rows, D_MODEL), tile),
            whole(w_out), whole(g1), whole(b1),
            whole(w_up), whole(conv_w), whole(conv_b), whole(w_down), whole(g2), whole(b2),
        ],
        out_specs=pl.BlockSpec((rows, D_MODEL), tile),
        scratch_shapes=[
            pltpu.VMEM((rows, D_MODEL), F32),
            pltpu.VMEM((n_chunks, CONV_TAIL + rows, FFN_CHUNK), F32),
            pltpu.VMEM((n_chunks, CONV_TAIL + rows, FFN_CHUNK), F32),
            pltpu.VMEM((rows, D_FF), BF16),
        ],
        compiler_params=pltpu.CompilerParams(
            dimension_semantics=("arbitrary",), vmem_limit_bytes=VMEM_LIMIT_BYTES),
        name="mix_ffn",
    )(sb_o, gla_o, x, w_out, g1, b1, w_up, conv_w, conv_b, w_down, g2, b2)


def _layer(x, w_in, layer, gate_up, gate_bias, gla_norm_g, w_out, ln1_g, ln1_b,
           w_up, conv_w, conv_b, w_down, ln2_g, ln2_b, alpha, batch, seq):
    row = lambda t: t.reshape(1, -1)
    q, k, v, gq, gk, gv, gg, log_a = _input_projection(
        x, w_in, layer, gate_up, row(gate_bias))
    sb_o = _stick_breaking(q, k, v, batch, seq)
    gla_o = _gla(gq, gk, gv, gg, log_a, row(gla_norm_g), batch, seq)
    return _mix_ffn(sb_o, gla_o, x, w_out.astype(BF16), row(ln1_g), row(ln1_b),
                    w_up.astype(BF16), conv_w, row(conv_b), w_down.astype(BF16),
                    row(ln2_g), row(ln2_b), alpha, seq)


def kernel(x, w_in, gate_up, gate_bias, gla_norm_g, w_out, ln1_g, ln1_b, w_up, conv_w, conv_b,
           w_down, ln2_g, ln2_b):
    batch, seq, d_model = x.shape
    depth = w_in.shape[0]
    alpha = (2.0 * depth) ** 0.25
    y = x.reshape(batch * seq, d_model)
    for l in range(depth):
        y = _layer(y, w_in, l, gate_up[l], gate_bias[l], gla_norm_g[l], w_out[l], ln1_g[l], ln1_b[l],
                   w_up[l], conv_w[l], conv_b[l], w_down[l], ln2_g[l], ln2_b[l], alpha, batch, seq)
    return y.reshape(batch, seq, d_model)
```
